```python
import jax
import jax.numpy as jnp
from jax import lax
import numpy as np


D_MODEL = 4096
BATCH = 1
SEQ = 8192
DEPTH = 4

GRID_W = 64
CTX_LEN = 256
HEAD_DIM = 128
N_Q_HEADS = (D_MODEL // 2) // HEAD_DIM
N_KV_HEADS = max(N_Q_HEADS // 4, 1)
GQA_GROUP = N_Q_HEADS // N_KV_HEADS
ATTN_W = N_Q_HEADS * HEAD_DIM
KV_W = N_KV_HEADS * HEAD_DIM
WINDOW = 128
BLOCK = 128
CHUNK = 128
GMLP_GROUP_W = 128
GMLP_GROUPS = (D_MODEL // 2) // GMLP_GROUP_W
GMLP_W = GMLP_GROUPS * GMLP_GROUP_W
AB_SPLITS = (ATTN_W, KV_W, KV_W, ATTN_W, GMLP_W, GMLP_W, GMLP_W)
AB_IN_W = 2 * ATTN_W + 2 * KV_W + 3 * GMLP_W
AB_OUT_W = ATTN_W + GMLP_W
CONV_W = D_MODEL
CONV_K = 31
C_IN_W = 3 * CONV_W
ROPE_BASE = 10000.0
RMS_EPS = 1e-6
LN_EPS = 1e-5
NEG_INF = -1e30

kernel_name = 'hybrid_swa_gmlp_conformer_dit'


def _split(t, sizes):
    idx = []
    acc = 0
    for s in sizes[:-1]:
        acc += s
        idx.append(acc)
    return jnp.split(t, idx, axis=-1)


def rmsnorm(x, g):
    xf = x.astype(jnp.float32)
    y = xf * lax.rsqrt(jnp.mean(xf * xf, axis=-1, keepdims=True) + RMS_EPS)
    return (y * g.astype(jnp.float32)).astype(x.dtype)


def layernorm(x, g, b):
    xf = x.astype(jnp.float32)
    mu = jnp.mean(xf, axis=-1, keepdims=True)
    var = jnp.mean(jnp.square(xf - mu), axis=-1, keepdims=True)
    y = (xf - mu) * lax.rsqrt(var + LN_EPS)
    return (y * g.astype(jnp.float32) + b.astype(jnp.float32)).astype(x.dtype)


def axial_rope(x, row, col):
    half = HEAD_DIM // 2
    quarter = HEAD_DIM // 4
    inv = 1.0 / (ROPE_BASE ** (jnp.arange(quarter, dtype=jnp.float32) / quarter))

    def rot(xp, pos):
        ang = pos.astype(jnp.float32)[:, None] * inv[None, :]
        cos = jnp.cos(ang)[None, :, None, :].astype(x.dtype)
        sin = jnp.sin(ang)[None, :, None, :].astype(x.dtype)
        a, b = xp[..., :quarter], xp[..., quarter:]
        return jnp.concatenate([a * cos - b * sin, b * cos + a * sin], axis=-1)

    return jnp.concatenate([rot(x[..., :half], row), rot(x[..., half:], col)], axis=-1)


def band_blocks(t):
    B, S, H, D = t.shape
    nb = S // BLOCK
    tp = jnp.pad(t, ((0, 0), (BLOCK, BLOCK), (0, 0), (0, 0))).reshape(B, nb + 2, BLOCK, H, D)
    return jnp.concatenate([tp[:, :-2], tp[:, 1:-1], tp[:, 2:]], axis=2)


def windowed_attention(q, k, v, kc, vc, sink):
    B, S, _, _ = q.shape
    nb = S // BLOCK
    n_band = 3 * BLOCK
    scale = HEAD_DIM ** -0.5
    qb = q.reshape(B, nb, BLOCK, N_KV_HEADS, GQA_GROUP, HEAD_DIM)
    kb = band_blocks(k)
    vb = band_blocks(v)
    s_band = jnp.einsum('bnqhgd,bnkhd->bnhgqk', qb, kb).astype(jnp.float32) * scale
    s_ctx = jnp.einsum('bnqhgd,bkhd->bnhgqk', qb, kc).astype(jnp.float32) * scale
    qpos = jnp.arange(nb)[:, None] * BLOCK + jnp.arange(BLOCK)[None, :]
    kpos = (jnp.arange(nb)[:, None] - 1) * BLOCK + jnp.arange(n_band)[None, :]
    valid = ((jnp.abs(qpos[:, :, None] - kpos[:, None, :]) <= WINDOW)
             & (kpos[:, None, :] >= 0) & (kpos[:, None, :] < S))
    s_band = jnp.where(valid[None, :, None, None, :, :], s_band, NEG_INF)
    sink_l = jnp.broadcast_to(
        sink.astype(jnp.float32).reshape(N_KV_HEADS, GQA_GROUP)[None, None, :, :, None, None],
        s_band.shape[:-1] + (1,))
    p = jax.nn.softmax(jnp.concatenate([s_band, s_ctx, sink_l], axis=-1), axis=-1).astype(v.dtype)
    n_ctx = kc.shape[1]
    o = (jnp.einsum('bnhgqk,bnkhd->bnqhgd', p[..., :n_band], vb)
         + jnp.einsum('bnhgqk,bkhd->bnqhgd', p[..., n_band:n_band + n_ctx], vc))
    return o.reshape(B, S, ATTN_W)


def context_attention(q, k, v, sink):
    B, C, _, _ = q.shape
    scale = HEAD_DIM ** -0.5
    qg = q.reshape(B, C, N_KV_HEADS, GQA_GROUP, HEAD_DIM)
    s = jnp.einsum('bqhgd,bkhd->bhgqk', qg, k).astype(jnp.float32) * scale
    sink_l = jnp.broadcast_to(
        sink.astype(jnp.float32).reshape(N_KV_HEADS, GQA_GROUP)[None, :, :, None, None],
        s.shape[:-1] + (1,))
    p = jax.nn.softmax(jnp.concatenate([s, sink_l], axis=-1), axis=-1)[..., :C].astype(v.dtype)
    o = jnp.einsum('bhgqk,bkhd->bqhgd', p, v)
    return o.reshape(B, C, ATTN_W)


def chunk_gmlp(u, v, ln_g, ln_b, ws, ws_b):
    u = jax.nn.gelu(u, approximate=False)
    v = layernorm(jax.nn.gelu(v, approximate=False), ln_g, ln_b)
    B, L, _ = v.shape
    nc = L // CHUNK
    vch = v.reshape(B, nc, CHUNK, GMLP_GROUPS, GMLP_GROUP_W)
    s = jnp.einsum('gpq,bnqgc->bnpgc', ws, vch) + ws_b.T[None, None, :, :, None]
    return u * s.reshape(B, L, GMLP_W)


def conformer_conv(a, b, dw, dw_b, ln_g, ln_b):
    glu = a * jax.nn.sigmoid(b)
    y = lax.conv_general_dilated(
        glu, dw[:, None, :], window_strides=(1,), padding=((CONV_K // 2, CONV_K // 2),),
        dimension_numbers=('NWC', 'WIO', 'NWC'), feature_group_count=CONV_W) + dw_b
    return jax.nn.silu(layernorm(y, ln_g, ln_b))


def setup_inputs(seed: int = 0) -> dict:
    key = jax.random.key(seed)
    ks = jax.random.split(key, 24)
    n_even = (DEPTH + 1) // 2
    n_odd = DEPTH // 2

    def nrm(k, shape, s):
        return jax.random.normal(k, shape, jnp.float32) * s

    return {
        'x': nrm(ks[0], (BATCH, SEQ, D_MODEL), 1.0),
        'c': nrm(ks[1], (BATCH, D_MODEL), 1.0),
        'ctx': nrm(ks[2], (BATCH, CTX_LEN, D_MODEL), 1.0),
        'c_ctx': nrm(ks[3], (D_MODEL,), 1.0),
        'ada_w': nrm(ks[4], (DEPTH, D_MODEL, 3 * D_MODEL), 0.5 * D_MODEL ** -0.5),
        'ada_b': nrm(ks[5], (DEPTH, 3 * D_MODEL), 0.02),
        'pre_g': 1.0 + nrm(ks[6], (DEPTH, D_MODEL), 0.02),
        'post_g': 1.0 + nrm(ks[7], (DEPTH, D_MODEL), 0.02),
        'ab_w_in': nrm(ks[8], (n_even, D_MODEL, AB_IN_W), D_MODEL ** -0.5),
        'ab_sink': nrm(ks[9], (n_even, N_Q_HEADS), 0.5),
        'ab_ln_g': 1.0 + nrm(ks[10], (n_even, GMLP_W), 0.02),
        'ab_ln_b': nrm(ks[11], (n_even, GMLP_W), 0.02),
        'ab_ws': nrm(ks[12], (n_even, GMLP_GROUPS, CHUNK, CHUNK), CHUNK ** -0.5),
        'ab_ws_b': 1.0 + nrm(ks[13], (n_even, GMLP_GROUPS, CHUNK), 0.02),
        'ab_w_out': nrm(ks[14], (n_even, AB_OUT_W, D_MODEL), AB_OUT_W ** -0.5),
        'cv_w_in': nrm(ks[15], (n_odd, D_MODEL, C_IN_W), D_MODEL ** -0.5),
        'cv_dw': nrm(ks[16], (n_odd, CONV_K, CONV_W), CONV_K ** -0.5),
        'cv_dw_b': nrm(ks[17], (n_odd, CONV_W), 0.02),
        'cv_ln_g': 1.0 + nrm(ks[18], (n_odd, CONV_W), 0.02),
        'cv_ln_b': nrm(ks[19], (n_odd, CONV_W), 0.02),
        'cv_w_out': nrm(ks[20], (n_odd, CONV_W, D_MODEL), CONV_W ** -0.5),
    }


def reference(x, c, ctx, c_ctx, ada_w, ada_b, pre_g, post_g, ab_w_in, ab_sink, ab_ln_g, ab_ln_b,
              ab_ws, ab_ws_b, ab_w_out, cv_w_in, cv_dw, cv_dw_b, cv_ln_g, cv_ln_b, cv_w_out):
    B, S, D = x.shape
    Lc = ctx.shape[1]
    ROWS = S // GRID_W
    row = jnp.repeat(jnp.arange(ROWS, dtype=jnp.int32), GRID_W)
    col = jnp.tile(jnp.arange(GRID_W, dtype=jnp.int32), ROWS)

    for l in range(DEPTH):
        last = l == DEPTH - 1
        even = l % 2 == 0
        need_ctx = even or (not last)
        i = l // 2

        mod = jax.nn.silu(c) @ ada_w[l] + ada_b[l]
        shift, scale, gate = jnp.split(mod, 3, axis=-1)
        h = rmsnorm(x, pre_g[l]) * (1.0 + scale[:, None, :]) + shift[:, None, :]
        if need_ctx:
            mod_c = jax.nn.silu(c_ctx) @ ada_w[l] + ada_b[l]
            shift_c, scale_c, gate_c = jnp.split(mod_c, 3, axis=-1)
            hc = rmsnorm(ctx, pre_g[l]) * (1.0 + scale_c) + shift_c

        if even:
            q, k, v, ga, u, vg, gb = _split(h @ ab_w_in[i], AB_SPLITS)
            q = axial_rope(q.reshape(B, S, N_Q_HEADS, HEAD_DIM), row, col)
            k = axial_rope(k.reshape(B, S, N_KV_HEADS, HEAD_DIM), row, col)
            v = v.reshape(B, S, N_KV_HEADS, HEAD_DIM)
            qc, kc, vc, gac, uc, vgc, gbc = _split(hc @ ab_w_in[i], AB_SPLITS)
            kc = kc.reshape(B, Lc, N_KV_HEADS, HEAD_DIM)
            vc = vc.reshape(B, Lc, N_KV_HEADS, HEAD_DIM)
            attn = windowed_attention(q, k, v, kc, vc, ab_sink[i])
            mix = chunk_gmlp(u, vg, ab_ln_g[i], ab_ln_b[i], ab_ws[i], ab_ws_b[i])
            y = jnp.concatenate([attn * jax.nn.silu(ga), mix * jax.nn.silu(gb)], axis=-1) @ ab_w_out[i]
            if not last:
                attn_c = context_attention(qc.reshape(B, Lc, N_Q_HEADS, HEAD_DIM), kc, vc, ab_sink[i])
                mix_c = chunk_gmlp(uc, vgc, ab_ln_g[i], ab_ln_b[i], ab_ws[i], ab_ws_b[i])
                yc = jnp.concatenate([attn_c * jax.nn.silu(gac), mix_c * jax.nn.silu(gbc)], axis=-1) @ ab_w_out[i]
        else:
            a, b, g = jnp.split(h @ cv_w_in[i], 3, axis=-1)
            y = (conformer_conv(a, b, cv_dw[i], cv_dw_b[i], cv_ln_g[i], cv_ln_b[i]) * jax.nn.silu(g)) @ cv_w_out[i]
            if not last:
                ac, bc, gcv = jnp.split(hc @ cv_w_in[i], 3, axis=-1)
                yc = (conformer_conv(ac, bc, cv_dw[i], cv_dw_b[i], cv_ln_g[i], cv_ln_b[i])
                      * jax.nn.silu(gcv)) @ cv_w_out[i]

        x = x + gate[:, None, :] * rmsnorm(y, post_g[l])
        if not last:
            ctx = ctx + gate_c * rmsnorm(yc, post_g[l])

    return x
```

```python
import functools

import jax
import jax.numpy as jnp
import numpy as np
from jax import lax
from jax.experimental import pallas as pl
from jax.experimental.pallas import tpu as pltpu

D_MODEL = 4096
SEQ = 8192
DEPTH = 4
GRID_W = 64
CTX_LEN = 256
ROWS = SEQ + CTX_LEN
HEAD_DIM = 128
N_Q_HEADS = 16
N_KV_HEADS = 4
GQA_GROUP = 4
ATTN_W = 2048
KV_W = 512
BLOCK = 128
N_BLOCKS = SEQ // BLOCK
GMLP_GROUPS = 16
GMLP_W = 2048
AB_IN_W = 11264
CONV_K = 31
CONV_HALO = 16
ROPE_BASE = 10000.0
RMS_EPS = 1e-6
LN_EPS = 1e-5
NEG_INF = -1e30
EPI_ROWS = 16

F32 = jnp.float32
BF16 = jnp.bfloat16
MIB = 1024 * 1024

IN_TN = 512
J_Q = (0, 4)
J_GA = (4, 8)
J_U = (8, 12)
J_VG = (12, 16)
J_GB = (16, 20)
J_K = 20
J_V = 21


def _params(semantics, vmem_mib):
    return pltpu.CompilerParams(dimension_semantics=semantics, vmem_limit_bytes=vmem_mib * MIB)


def _silu(t):
    return t * jax.nn.sigmoid(t)


def _gelu(t):
    return 0.5 * t * (1.0 + lax.erf(t * np.float32(np.sqrt(0.5))))


def _mod_kernel(cc_ref, w_ref, b_ref, o_ref):
    a = _silu(cc_ref[...]).astype(BF16)
    w = w_ref[...].astype(BF16)
    o_ref[...] = jnp.dot(a, w, preferred_element_type=F32) + b_ref[...]


def _modulation(cc, ada_w, ada_b):
    tn = 512
    n3 = 3 * D_MODEL
    return pl.pallas_call(
        _mod_kernel,
        grid=(DEPTH, n3 // tn),
        in_specs=[
            pl.BlockSpec((8, D_MODEL), lambda l, j: (0, 0)),
            pl.BlockSpec((None, D_MODEL, tn), lambda l, j: (l, 0, j)),
            pl.BlockSpec((None, 1, tn), lambda l, j: (l, 0, j)),
        ],
        out_specs=pl.BlockSpec((None, 8, tn), lambda l, j: (l, 0, j)),
        out_shape=jax.ShapeDtypeStruct((DEPTH, 8, n3), F32),
        compiler_params=_params(("arbitrary", "arbitrary"), 40),
        name="adaln_mod",
    )(cc, ada_w, ada_b.reshape(DEPTH, 1, n3))


def _prenorm_kernel(x_ref, mod_ref, g_ref, o_ref, *, tr):
    is_ctx = pl.program_id(0) * tr >= SEQ
    x = x_ref[...]
    ms = jnp.mean(x * x, axis=-1, keepdims=True)
    y = x * lax.rsqrt(ms + RMS_EPS) * g_ref[...]
    shift = jnp.where(is_ctx, mod_ref[1:2, 0:D_MODEL], mod_ref[0:1, 0:D_MODEL])
    scale = jnp.where(is_ctx, mod_ref[1:2, D_MODEL:2 * D_MODEL], mod_ref[0:1, D_MODEL:2 * D_MODEL])
    o_ref[...] = (y * (1.0 + scale) + shift).astype(BF16)


def _prenorm(x_all, mod, pre_g, layer, rows):
    tr = 256
    return pl.pallas_call(
        functools.partial(_prenorm_kernel, tr=tr),
        grid=(rows // tr,),
        in_specs=[
            pl.BlockSpec((tr, D_MODEL), lambda i: (i, 0)),
            pl.BlockSpec((None, 8, 3 * D_MODEL), lambda i: (layer, 0, 0)),
            pl.BlockSpec((None, 1, D_MODEL), lambda i: (layer, 0, 0)),
        ],
        out_specs=pl.BlockSpec((tr, D_MODEL), lambda i: (i, 0)),
        out_shape=jax.ShapeDtypeStruct((rows, D_MODEL), BF16),
        compiler_params=_params(("arbitrary",), 40),
        name="prenorm",
    )(x_all, mod, pre_g.reshape(DEPTH, 1, D_MODEL))


def _inproj_even_kernel(h_ref, w_ref, c_ref, s1_ref, s2_ref, o_ref):
    j = pl.program_id(0)
    acc = jnp.dot(h_ref[...], w_ref[...], preferred_element_type=F32)

    @pl.when((j < J_Q[1]) | (j == J_K))
    def _():
        c = c_ref[...]
        s1 = s1_ref[...]
        s2 = s2_ref[...]
        for hh in range(IN_TN // HEAD_DIM):
            sl = slice(hh * HEAD_DIM, (hh + 1) * HEAD_DIM)
            xs = acc[:, sl]
            up = pltpu.roll(xs, HEAD_DIM - 32, axis=1)
            dn = pltpu.roll(xs, 32, axis=1)
            o_ref[:, sl] = (xs * c + up * s1 + dn * s2).astype(BF16)

    @pl.when(j == J_V)
    def _():
        o_ref[...] = acc.astype(BF16)

    @pl.when(((j >= J_GA[0]) & (j < J_GA[1])) | ((j >= J_GB[0]) & (j < J_GB[1])))
    def _():
        o_ref[...] = _silu(acc).astype(BF16)

    @pl.when((j >= J_U[0]) & (j < J_VG[1]))
    def _():
        o_ref[...] = _gelu(acc).astype(BF16)


def _inproj_even(h, w, rope_c, rope_s1, rope_s2, tm):
    rows = h.shape[0]
    return pl.pallas_call(
        _inproj_even_kernel,
        grid=(AB_IN_W // IN_TN, rows // tm),
        in_specs=[
            pl.BlockSpec((tm, D_MODEL), lambda j, i: (i, 0)),
            pl.BlockSpec((D_MODEL, IN_TN), lambda j, i: (0, j)),
            pl.BlockSpec((tm, HEAD_DIM), lambda j, i: (i, 0)),
            pl.BlockSpec((tm, HEAD_DIM), lambda j, i: (i, 0)),
            pl.BlockSpec((tm, HEAD_DIM), lambda j, i: (i, 0)),
        ],
        out_specs=pl.BlockSpec((tm, IN_TN), lambda j, i: (i, j)),
        out_shape=jax.ShapeDtypeStruct((rows, AB_IN_W), BF16),
        compiler_params=_params(("arbitrary", "arbitrary"), 48),
        name="inproj_even",
    )(h, w, rope_c, rope_s1, rope_s2)


def _dot_nt(a, b):
    return lax.dot_general(a, b, (((1,), (1,)), ((), ())), preferred_element_type=F32)


def _attn_kernel(q_ref, ga_ref, kp_ref, km_ref, kn_ref, vp_ref, vm_ref, vn_ref, kc_ref, vc_ref,
                 sink_ref, o_ref):
    n = pl.program_id(0)
    scale = np.float32(HEAD_DIM ** -0.5)
    ri = lax.broadcasted_iota(jnp.int32, (BLOCK, BLOCK), 0)
    cj = lax.broadcasted_iota(jnp.int32, (BLOCK, BLOCK), 1)

    def finish(hq, o, l):
        sl = slice(hq * HEAD_DIM, (hq + 1) * HEAD_DIM)
        o_ref[:, sl] = ((o / l) * ga_ref[:, sl].astype(F32)).astype(BF16)

    @pl.when(n < N_BLOCKS)
    def _():
        mask_p = (cj >= ri) & (n > 0)
        mask_n = (cj <= ri) & (n < N_BLOCKS - 1)
        for h in range(N_KV_HEADS):
            ks = slice(h * HEAD_DIM, (h + 1) * HEAD_DIM)
            kp, km, kn, kc = kp_ref[:, ks], km_ref[:, ks], kn_ref[:, ks], kc_ref[:, ks]
            vp, vm, vn, vc = vp_ref[:, ks], vm_ref[:, ks], vn_ref[:, ks], vc_ref[:, ks]
            for g in range(GQA_GROUP):
                hq = h * GQA_GROUP + g
                q = q_ref[:, hq * HEAD_DIM:(hq + 1) * HEAD_DIM]
                sp = jnp.where(mask_p, _dot_nt(q, kp) * scale, NEG_INF)
                sm = _dot_nt(q, km) * scale
                sn = jnp.where(mask_n, _dot_nt(q, kn) * scale, NEG_INF)
                sc = _dot_nt(q, kc) * scale
                snk = sink_ref[hq:hq + 1, 0:1]
                m = jnp.maximum(jnp.maximum(jnp.max(sp, axis=-1, keepdims=True),
                                            jnp.max(sm, axis=-1, keepdims=True)),
                                jnp.maximum(jnp.max(sn, axis=-1, keepdims=True),
                                            jnp.max(sc, axis=-1, keepdims=True)))
                m = jnp.maximum(m, snk)
                pp = jnp.exp(sp - m)
                pm = jnp.exp(sm - m)
                pn = jnp.exp(sn - m)
                pc = jnp.exp(sc - m)
                l = (jnp.sum(pp, axis=-1, keepdims=True) + jnp.sum(pm, axis=-1, keepdims=True)
                     + jnp.sum(pn, axis=-1, keepdims=True) + jnp.sum(pc, axis=-1, keepdims=True)
                     + jnp.exp(snk - m))
                o = (jnp.dot(pp.astype(BF16), vp, preferred_element_type=F32)
                     + jnp.dot(pm.astype(BF16), vm, preferred_element_type=F32)
                     + jnp.dot(pn.astype(BF16), vn, preferred_element_type=F32)
                     + jnp.dot(pc.astype(BF16), vc, preferred_element_type=F32))
                finish(hq, o, l)

    @pl.when(n >= N_BLOCKS)
    def _():
        for h in range(N_KV_HEADS):
            ks = slice(h * HEAD_DIM, (h + 1) * HEAD_DIM)
            kc, vc = kc_ref[:, ks], vc_ref[:, ks]
            for g in range(GQA_GROUP):
                hq = h * GQA_GROUP + g
                q = q_ref[:, hq * HEAD_DIM:(hq + 1) * HEAD_DIM]
                sc = _dot_nt(q, kc) * scale
                snk = sink_ref[hq:hq + 1, 0:1]
                m = jnp.maximum(jnp.max(sc, axis=-1, keepdims=True), snk)
                pc = jnp.exp(sc - m)
                l = jnp.sum(pc, axis=-1, keepdims=True) + jnp.exp(snk - m)
                o = jnp.dot(pc.astype(BF16), vc, preferred_element_type=F32)
                finish(hq, o, l)


def _attention(proj, sink_b):
    rows = proj.shape[0]
    nb = rows // BLOCK
    kcol = J_K * IN_TN // KV_W
    vcol = J_V * IN_TN // KV_W
    ctx_blk = SEQ // CTX_LEN
    lo = lambda n: jnp.clip(n - 1, 0, N_BLOCKS - 1)
    mid = lambda n: jnp.minimum(n, N_BLOCKS - 1)
    hi = lambda n: jnp.clip(n + 1, 0, N_BLOCKS - 1)
    kv = lambda f, col: pl.BlockSpec((BLOCK, KV_W), lambda n: (f(n), col))
    return pl.pallas_call(
        _attn_kernel,
        grid=(nb,),
        in_specs=[
            pl.BlockSpec((BLOCK, ATTN_W), lambda n: (n, 0)),
            pl.BlockSpec((BLOCK, ATTN_W), lambda n: (n, 1)),
            kv(lo, kcol), kv(mid, kcol), kv(hi, kcol),
            kv(lo, vcol), kv(mid, vcol), kv(hi, vcol),
            pl.BlockSpec((CTX_LEN, KV_W), lambda n: (ctx_blk, kcol)),
            pl.BlockSpec((CTX_LEN, KV_W), lambda n: (ctx_blk, vcol)),
            pl.BlockSpec((N_Q_HEADS, HEAD_DIM), lambda n: (0, 0)),
        ],
        out_specs=pl.BlockSpec((BLOCK, ATTN_W), lambda n: (n, 0)),
        out_shape=jax.ShapeDtypeStruct((rows, ATTN_W), BF16),
        compiler_params=_params(("arbitrary",), 40),
        name="band_attention",
    )(proj, proj, proj, proj, proj, proj, proj, proj, proj, proj, sink_b)


def _gmlp_kernel(u_ref, vg_ref, gb_ref, ws_ref, wsb_ref, lg_ref, lb_ref, o_ref):
    v = vg_ref[...].astype(F32)
    mu = jnp.mean(v, axis=-1, keepdims=True)
    vc = v - mu
    var = jnp.mean(vc * vc, axis=-1, keepdims=True)
    y = (vc * lax.rsqrt(var + LN_EPS) * lg_ref[...] + lb_ref[...]).astype(BF16)
    for g in range(GMLP_GROUPS):
        sl = slice(g * HEAD_DIM, (g + 1) * HEAD_DIM)
        s = jnp.dot(ws_ref[g], y[:, sl], preferred_element_type=F32) + wsb_ref[g]
        o_ref[:, sl] = (u_ref[:, sl].astype(F32) * s * gb_ref[:, sl].astype(F32)).astype(BF16)


def _gmlp(proj, ws, wsb_b, ln_g, ln_b):
    rows = proj.shape[0]
    col = lambda c: pl.BlockSpec((BLOCK, GMLP_W), lambda n: (n, c))
    return pl.pallas_call(
        _gmlp_kernel,
        grid=(rows // BLOCK,),
        in_specs=[
            col(J_U[0] * IN_TN // GMLP_W), col(J_VG[0] * IN_TN // GMLP_W), col(J_GB[0] * IN_TN // GMLP_W),
            pl.BlockSpec((GMLP_GROUPS, BLOCK, BLOCK), lambda n: (0, 0, 0)),
            pl.BlockSpec((GMLP_GROUPS, BLOCK, BLOCK), lambda n: (0, 0, 0)),
            pl.BlockSpec((1, GMLP_W), lambda n: (0, 0)),
            pl.BlockSpec((1, GMLP_W), lambda n: (0, 0)),
        ],
        out_specs=pl.BlockSpec((BLOCK, GMLP_W), lambda n: (n, 0)),
        out_shape=jax.ShapeDtypeStruct((rows, GMLP_W), BF16),
        compiler_params=_params(("arbitrary",), 40),
        name="chunk_gmlp",
    )(proj, proj, proj, ws, wsb_b, ln_g, ln_b)


def _outproj_kernel(a1_ref, a2_ref, w_ref, x_ref, mod_ref, pg_ref, o_ref, *, nk, tm):
    i = pl.program_id(0)
    k = pl.program_id(1)
    half = nk // 2

    @pl.when(k == 0)
    def _():
        o_ref[...] = jnp.dot(a1_ref[...], w_ref[...], preferred_element_type=F32)

    @pl.when((k > 0) & (k < half))
    def _():
        o_ref[...] += jnp.dot(a1_ref[...], w_ref[...], preferred_element_type=F32)

    @pl.when(k >= half)
    def _():
        o_ref[...] += jnp.dot(a2_ref[...], w_ref[...], preferred_element_type=F32)

    @pl.when(k == nk - 1)
    def _():
        def row_chunk(r, carry):
            rs = pl.ds(pl.multiple_of(r * EPI_ROWS, EPI_ROWS), EPI_ROWS)
            y = o_ref[rs, :]
            ms = jnp.mean(y * y, axis=-1, keepdims=True)
            yn = y * lax.rsqrt(ms + RMS_EPS) * pg_ref[...]
            row = i * tm + r * EPI_ROWS + lax.broadcasted_iota(jnp.int32, (EPI_ROWS, 1), 0)
            gate = jnp.where(row >= SEQ, mod_ref[1:2, 2 * D_MODEL:3 * D_MODEL],
                             mod_ref[0:1, 2 * D_MODEL:3 * D_MODEL])
            o_ref[rs, :] = x_ref[rs, :] + gate * yn
            return carry

        lax.fori_loop(0, tm // EPI_ROWS, row_chunk, 0)


def _outproj(a1, a2, a2_col0, w, x_all, mod, post_g, layer, rows, tm):
    tk = 512
    nk = D_MODEL // tk
    half = nk // 2
    return pl.pallas_call(
        functools.partial(_outproj_kernel, nk=nk, tm=tm),
        grid=(rows // tm, nk),
        in_specs=[
            pl.BlockSpec((tm, tk), lambda i, k: (i, jnp.minimum(k, half - 1))),
            pl.BlockSpec((tm, tk), lambda i, k: (i, jnp.maximum(k - half, 0) + a2_col0)),
            pl.BlockSpec((tk, D_MODEL), lambda i, k: (k, 0)),
            pl.BlockSpec((tm, D_MODEL), lambda i, k: (i, 0)),
            pl.BlockSpec((None, 8, 3 * D_MODEL), lambda i, k: (layer, 0, 0)),
            pl.BlockSpec((None, 1, D_MODEL), lambda i, k: (layer, 0, 0)),
        ],
        out_specs=pl.BlockSpec((tm, D_MODEL), lambda i, k: (i, 0)),
        out_shape=jax.ShapeDtypeStruct((rows, D_MODEL), F32),
        compiler_params=_params(("arbitrary", "arbitrary"), 56),
        name="outproj_postnorm",
    )(a1, a2, w, x_all, mod, post_g.reshape(DEPTH, 1, D_MODEL))


def _inproj_conv_kernel(h_ref, wa_ref, wb_ref, wg_ref, glu_ref, sg_ref):
    h = h_ref[...]
    a = jnp.dot(h, wa_ref[...], preferred_element_type=F32)
    b = jnp.dot(h, wb_ref[...], preferred_element_type=F32)
    g = jnp.dot(h, wg_ref[...], preferred_element_type=F32)
    glu_ref[...] = (a * jax.nn.sigmoid(b)).astype(BF16)
    sg_ref[...] = _silu(g).astype(BF16)


def _inproj_conv(h, w, tm):
    rows = h.shape[0]
    tn = 512
    nj = D_MODEL // tn
    wspec = lambda part: pl.BlockSpec((D_MODEL, tn), lambda j, i: (0, j + part * nj))
    out = pl.BlockSpec((tm, tn), lambda j, i: (i, j))
    return pl.pallas_call(
        _inproj_conv_kernel,
        grid=(nj, rows // tm),
        in_specs=[pl.BlockSpec((tm, D_MODEL), lambda j, i: (i, 0)), wspec(0), wspec(1), wspec(2)],
        out_specs=[out, out],
        out_shape=[jax.ShapeDtypeStruct((rows, D_MODEL), BF16)] * 2,
        compiler_params=_params(("arbitrary", "arbitrary"), 56),
        name="inproj_conv",
    )(h, w, w, w)


def _conv_kernel(gp_ref, gm_ref, gn_ref, sg_ref, dw_ref, dwb_ref, lg_ref, lb_ref, o_ref,
                 buf_ref, y_ref, *, tr):
    i = pl.program_id(0)
    first_ctx = SEQ // tr
    has_prev = (i != 0) & (i != first_ctx)
    has_next = (i != first_ctx - 1) & (i != pl.num_programs(0) - 1)
    buf_ref[0:CONV_HALO, :] = jnp.where(has_prev, gp_ref[...].astype(F32), 0.0)
    buf_ref[CONV_HALO:CONV_HALO + tr, :] = gm_ref[...].astype(F32)
    buf_ref[CONV_HALO + tr:2 * CONV_HALO + tr, :] = jnp.where(has_next, gn_ref[...].astype(F32), 0.0)
    base = CONV_HALO - CONV_K // 2

    def lane_block(c, carry):
        ls = pl.ds(pl.multiple_of(c * 128, 128), 128)
        acc = jnp.broadcast_to(dwb_ref[:, ls], (tr, 128))
        for d in range(CONV_K):
            acc = acc + buf_ref[pl.ds(base + d, tr), ls] * dw_ref[d:d + 1, ls]
        y_ref[:, ls] = acc
        return carry

    lax.fori_loop(0, D_MODEL // 128, lane_block, 0)
    y = y_ref[...]
    mu = jnp.mean(y, axis=-1, keepdims=True)
    yc = y - mu
    var = jnp.mean(yc * yc, axis=-1, keepdims=True)
    z = yc * lax.rsqrt(var + LN_EPS) * lg_ref[...] + lb_ref[...]
    o_ref[...] = (_silu(z) * sg_ref[...].astype(F32)).astype(BF16)


def _conv_mix(glu, sg, dw, dw_b, ln_g, ln_b, rows):
    tr = 256
    per = tr // CONV_HALO
    n_halo = glu.shape[0] // CONV_HALO
    vec = pl.BlockSpec((1, D_MODEL), lambda i: (0, 0))
    return pl.pallas_call(
        functools.partial(_conv_kernel, tr=tr),
        grid=(rows // tr,),
        in_specs=[
            pl.BlockSpec((CONV_HALO, D_MODEL), lambda i: (jnp.maximum(i * per - 1, 0), 0)),
            pl.BlockSpec((tr, D_MODEL), lambda i: (i, 0)),
            pl.BlockSpec((CONV_HALO, D_MODEL), lambda i: (jnp.minimum((i + 1) * per, n_halo - 1), 0)),
            pl.BlockSpec((tr, D_MODEL), lambda i: (i, 0)),
            pl.BlockSpec((CONV_K + 1, D_MODEL), lambda i: (0, 0)),
            vec, vec, vec,
        ],
        out_specs=pl.BlockSpec((tr, D_MODEL), lambda i: (i, 0)),
        out_shape=jax.ShapeDtypeStruct((rows, D_MODEL), BF16),
        scratch_shapes=[
            pltpu.VMEM((tr + 2 * CONV_HALO, D_MODEL), F32),
            pltpu.VMEM((tr, D_MODEL), F32),
        ],
        compiler_params=_params(("arbitrary",), 48),
        name="conv_ln_swish",
    )(glu, glu, glu, sg, dw, dw_b, ln_g, ln_b)


def _rope_tables():
    quarter = HEAD_DIM // 4
    inv = 1.0 / (ROPE_BASE ** (jnp.arange(quarter, dtype=F32) / quarter))
    pos = jnp.arange(SEQ, dtype=jnp.int32)
    row = (pos // GRID_W).astype(F32)[:, None] * inv[None, :]
    col = (pos % GRID_W).astype(F32)[:, None] * inv[None, :]
    zero = jnp.zeros_like(row)
    cos = jnp.concatenate([jnp.cos(row), jnp.cos(row), jnp.cos(col), jnp.cos(col)], axis=-1)
    s_up = jnp.concatenate([-jnp.sin(row), zero, -jnp.sin(col), zero], axis=-1)
    s_dn = jnp.concatenate([zero, jnp.sin(row), zero, jnp.sin(col)], axis=-1)
    pad = lambda t, v: jnp.concatenate([t, jnp.full((CTX_LEN, HEAD_DIM), v, F32)], axis=0)
    return pad(cos, 1.0), pad(s_up, 0.0), pad(s_dn, 0.0)


def kernel(x, c, ctx, c_ctx, ada_w, ada_b, pre_g, post_g, ab_w_in, ab_sink, ab_ln_g, ab_ln_b, ab_ws, ab_ws_b,
           ab_w_out, cv_w_in, cv_dw, cv_dw_b, cv_ln_g, cv_ln_b, cv_w_out):
    assert x.shape == (1, SEQ, D_MODEL) and ctx.shape == (1, CTX_LEN, D_MODEL)
    x_all = jnp.concatenate([x[0], ctx[0]], axis=0)
    cc = jnp.concatenate([c, c_ctx[None, :], jnp.zeros((6, D_MODEL), F32)], axis=0)
    mod = _modulation(cc, ada_w, ada_b)
    rope_c, rope_s1, rope_s2 = _rope_tables()

    for layer in range(DEPTH):
        last = layer == DEPTH - 1
        i = layer // 2
        rows = SEQ if last else ROWS
        tm_in = 512 if last else 704
        tm_out = 512 if last else 528
        h = _prenorm(x_all, mod, pre_g, layer, rows)
        if layer % 2 == 0:
            w = ab_w_in[i]
            q, k, v, ga, u, vg, gb = jnp.split(w, np.cumsum([ATTN_W, KV_W, KV_W, ATTN_W, GMLP_W, GMLP_W]), axis=-1)
            w_in = jnp.concatenate([q, ga, u, vg, gb, k, v], axis=-1).astype(BF16)
            proj = _inproj_even(h, w_in, rope_c, rope_s1, rope_s2, tm_in)
            sink_b = jnp.broadcast_to(ab_sink[i][:, None], (N_Q_HEADS, HEAD_DIM))
            attn = _attention(proj, sink_b)
            wsb_b = jnp.broadcast_to(ab_ws_b[i][:, :, None], (GMLP_GROUPS, BLOCK, BLOCK))
            mix = _gmlp(proj, ab_ws[i].astype(BF16), wsb_b, ab_ln_g[i][None, :], ab_ln_b[i][None, :])
            x_all = _outproj(attn, mix, 0, ab_w_out[i].astype(BF16), x_all, mod, post_g, layer, rows, tm_out)
        else:
            glu, sg = _inproj_conv(h, cv_w_in[i].astype(BF16), tm_in)
            dw = jnp.concatenate([cv_dw[i], jnp.zeros((1, D_MODEL), F32)], axis=0)
            z = _conv_mix(glu, sg, dw, cv_dw_b[i][None, :], cv_ln_g[i][None, :], cv_ln_b[i][None, :], rows)
            x_all = _outproj(z, z, D_MODEL // 2 // 512, cv_w_out[i].astype(BF16), x_all, mod, post_g, layer,
                             rows, tm_out)
    return x_all[None]
```

```python
import functools

import jax
import jax.numpy as jnp
import numpy as np
from jax import lax
from jax.experimental import pallas as pl
from jax.experimental.pallas import tpu as pltpu

D_MODEL = 4096
SEQ = 8192
DEPTH = 4
GRID_W = 64
CTX_LEN = 256
ROWS = SEQ + CTX_LEN
HEAD_DIM = 128
N_Q_HEADS = 16
N_KV_HEADS = 4
GQA_GROUP = 4
ATTN_W = 2048
KV_W = 512
BLOCK = 128
N_BLOCKS = SEQ // BLOCK
GMLP_GROUPS = 16
GMLP_W = 2048
AB_IN_W = 11264
CONV_K = 31
CONV_HALO = 16
ROPE_BASE = 10000.0
RMS_EPS = 1e-6
LN_EPS = 1e-5
NEG_INF = -1e30

F32 = jnp.float32
BF16 = jnp.bfloat16
MIB = 1024 * 1024
LANES = 128
SUBLANES = 8
MXU_N = 256

ROW_TILE = 256
CAST_ROWS = 256
TM_ALL = 704
TM_LATENT = 512

IN_TN = 512
J_Q = (0, 4)
J_GA = (4, 8)
J_U = (8, 12)
J_VG = (12, 16)
J_GB = (16, 20)
J_K = 20
J_V = 21


def _params(semantics, vmem_mib):
    return pltpu.CompilerParams(dimension_semantics=semantics, vmem_limit_bytes=vmem_mib * MIB)


def _silu(t):
    return t * jax.nn.sigmoid(t)


def _gelu(t):
    return 0.5 * t * (1.0 + lax.erf(t * np.float32(np.sqrt(0.5))))


def _rms(t):
    return t * lax.rsqrt(jnp.mean(t * t, axis=-1, keepdims=True) + RMS_EPS)


def _cast_weight_once(w_ref, wb_ref):
    @pl.when(pl.program_id(1) == 0)
    def _():
        def rows(r, carry):
            rs = pl.ds(pl.multiple_of(r * CAST_ROWS, CAST_ROWS), CAST_ROWS)
            wb_ref[rs, :] = w_ref[rs, :].astype(BF16)
            return carry
        lax.fori_loop(0, w_ref.shape[0] // CAST_ROWS, rows, 0)


def _mod_kernel(cc_ref, w_ref, b_ref, o_ref):
    a = _silu(cc_ref[...]).astype(BF16)
    w = w_ref[...].astype(BF16)
    o_ref[...] = jnp.dot(a, w, preferred_element_type=F32) + b_ref[...]


def _modulation(cc, ada_w, ada_b):
    tn = 512
    n3 = 3 * D_MODEL
    return pl.pallas_call(
        _mod_kernel,
        grid=(DEPTH, n3 // tn),
        in_specs=[
            pl.BlockSpec((8, D_MODEL), lambda l, j: (0, 0)),
            pl.BlockSpec((None, D_MODEL, tn), lambda l, j: (l, 0, j)),
            pl.BlockSpec((None, 1, tn), lambda l, j: (l, 0, j)),
        ],
        out_specs=pl.BlockSpec((None, 8, tn), lambda l, j: (l, 0, j)),
        out_shape=jax.ShapeDtypeStruct((DEPTH, 8, n3), F32),
        compiler_params=_params(("arbitrary", "arbitrary"), 40),
        name="adaln_mod",
    )(cc, ada_w, ada_b.reshape(DEPTH, 1, n3))


def _modulate(x, mod_ref, g_ref, is_ctx):
    shift = jnp.where(is_ctx, mod_ref[1:2, 0:D_MODEL], mod_ref[0:1, 0:D_MODEL])
    scale = jnp.where(is_ctx, mod_ref[1:2, D_MODEL:2 * D_MODEL], mod_ref[0:1, D_MODEL:2 * D_MODEL])
    return (_rms(x) * g_ref[...] * (1.0 + scale) + shift).astype(BF16)


def _residual(x, y_ref, mod_ref, pg_ref, is_ctx):
    gate = jnp.where(is_ctx, mod_ref[1:2, 2 * D_MODEL:3 * D_MODEL], mod_ref[0:1, 2 * D_MODEL:3 * D_MODEL])
    return x + gate * (_rms(y_ref[...].astype(F32)) * pg_ref[...])


def _first_prenorm_kernel(x_ref, ctx_ref, mod_ref, g_ref, xo_ref, h_ref):
    is_ctx = pl.program_id(0) * ROW_TILE >= SEQ
    x = jnp.where(is_ctx, ctx_ref[...], x_ref[...])
    xo_ref[...] = x
    h_ref[...] = _modulate(x, mod_ref, g_ref, is_ctx)


def _post_pre_kernel(x_ref, y_ref, modp_ref, pg_ref, mod_ref, g_ref, xo_ref, h_ref):
    is_ctx = pl.program_id(0) * ROW_TILE >= SEQ
    x = _residual(x_ref[...], y_ref, modp_ref, pg_ref, is_ctx)
    xo_ref[...] = x
    h_ref[...] = _modulate(x, mod_ref, g_ref, is_ctx)


def _final_post_kernel(x_ref, y_ref, modp_ref, pg_ref, xo_ref):
    xo_ref[...] = _residual(x_ref[...], y_ref, modp_ref, pg_ref, False)


def _row_spec():
    return pl.BlockSpec((ROW_TILE, D_MODEL), lambda i: (i, 0))


def _mod_spec(layer):
    return pl.BlockSpec((None, 8, 3 * D_MODEL), lambda i: (layer, 0, 0))


def _gain_spec(layer):
    return pl.BlockSpec((None, 1, D_MODEL), lambda i: (layer, 0, 0))


def _first_prenorm(x, ctx, mod, pre_g):
    last_x = SEQ // ROW_TILE - 1
    return pl.pallas_call(
        _first_prenorm_kernel,
        grid=(ROWS // ROW_TILE,),
        in_specs=[
            pl.BlockSpec((None, ROW_TILE, D_MODEL), lambda i: (0, jnp.minimum(i, last_x), 0)),
            pl.BlockSpec((None, ROW_TILE, D_MODEL), lambda i: (0, jnp.maximum(i - last_x - 1, 0), 0)),
            _mod_spec(0), _gain_spec(0),
        ],
        out_specs=[_row_spec(), _row_spec()],
        out_shape=[jax.ShapeDtypeStruct((ROWS, D_MODEL), F32), jax.ShapeDtypeStruct((ROWS, D_MODEL), BF16)],
        compiler_params=_params(("arbitrary",), 48),
        name="first_prenorm",
    )(x, ctx, mod, pre_g)


def _post_pre(x_all, y, mod, post_g, pre_g, layer, rows):
    return pl.pallas_call(
        _post_pre_kernel,
        grid=(rows // ROW_TILE,),
        in_specs=[_row_spec(), _row_spec(), _mod_spec(layer - 1), _gain_spec(layer - 1),
                  _mod_spec(layer), _gain_spec(layer)],
        out_specs=[_row_spec(), _row_spec()],
        out_shape=[jax.ShapeDtypeStruct((rows, D_MODEL), F32), jax.ShapeDtypeStruct((rows, D_MODEL), BF16)],
        compiler_params=_params(("arbitrary",), 48),
        name="post_pre_norm",
    )(x_all, y, mod, post_g, mod, pre_g)


def _final_post(x_all, y, mod, post_g, layer, rows):
    return pl.pallas_call(
        _final_post_kernel,
        grid=(rows // ROW_TILE,),
        in_specs=[_row_spec(), _row_spec(), _mod_spec(layer), _gain_spec(layer)],
        out_specs=_row_spec(),
        out_shape=jax.ShapeDtypeStruct((rows, D_MODEL), F32),
        compiler_params=_params(("arbitrary",), 48),
        name="final_post_norm",
    )(x_all, y, mod, post_g)


def _inproj_even_kernel(h_ref, w_ref, c_ref, s1_ref, s2_ref, o_ref, wb_ref):
    j = pl.program_id(0)
    _cast_weight_once(w_ref, wb_ref)

    def column_halves(epilogue):
        for hf in range(IN_TN // MXU_N):
            cs = slice(hf * MXU_N, (hf + 1) * MXU_N)
            acc = jnp.dot(h_ref[...], wb_ref[:, cs], preferred_element_type=F32)
            epilogue(acc, hf * MXU_N)

    def rope(acc, c0):
        c = c_ref[...]
        s1 = s1_ref[...]
        s2 = s2_ref[...]
        for hh in range(MXU_N // HEAD_DIM):
            xs = acc[:, hh * HEAD_DIM:(hh + 1) * HEAD_DIM]
            up = pltpu.roll(xs, HEAD_DIM - 32, axis=1)
            dn = pltpu.roll(xs, 32, axis=1)
            o_ref[:, c0 + hh * HEAD_DIM:c0 + (hh + 1) * HEAD_DIM] = (xs * c + up * s1 + dn * s2).astype(BF16)

    def store(fn):
        def epilogue(acc, c0):
            o_ref[:, c0:c0 + MXU_N] = fn(acc).astype(BF16)
        return epilogue

    @pl.when((j < J_Q[1]) | (j == J_K))
    def _():
        column_halves(rope)

    @pl.when(j == J_V)
    def _():
        column_halves(store(lambda t: t))

    @pl.when(((j >= J_GA[0]) & (j < J_GA[1])) | ((j >= J_GB[0]) & (j < J_GB[1])))
    def _():
        column_halves(store(_silu))

    @pl.when((j >= J_U[0]) & (j < J_VG[1]))
    def _():
        column_halves(store(_gelu))


def _even_weight_block(j):
    return jnp.where(j < J_Q[1], j, jnp.where(j < J_K, j + 2, j - (J_K - J_Q[1])))


def _inproj_even(h, w_in, layer_i, rope_c, rope_s1, rope_s2, tm):
    rows = h.shape[0]
    tbl = pl.BlockSpec((tm, HEAD_DIM), lambda j, i: (i, 0))
    return pl.pallas_call(
        _inproj_even_kernel,
        grid=(AB_IN_W // IN_TN, rows // tm),
        in_specs=[
            pl.BlockSpec((tm, D_MODEL), lambda j, i: (i, 0)),
            pl.BlockSpec((None, D_MODEL, IN_TN), lambda j, i: (layer_i, 0, _even_weight_block(j))),
            tbl, tbl, tbl,
        ],
        out_specs=pl.BlockSpec((tm, IN_TN), lambda j, i: (i, j)),
        out_shape=jax.ShapeDtypeStruct((rows, AB_IN_W), BF16),
        scratch_shapes=[pltpu.VMEM((D_MODEL, IN_TN), BF16)],
        compiler_params=_params(("arbitrary", "arbitrary"), 52),
        name="inproj_even",
    )(h, w_in, rope_c, rope_s1, rope_s2)


def _dot_nt(a, b):
    return lax.dot_general(a, b, (((1,), (1,)), ((), ())), preferred_element_type=F32)


def _attn_kernel(q_ref, ga_ref, kp_ref, km_ref, kn_ref, vp_ref, vm_ref, vn_ref, kc_ref, vc_ref,
                 sink_ref, o_ref):
    n = pl.program_id(0)
    scale = np.float32(HEAD_DIM ** -0.5)
    grp = GQA_GROUP * BLOCK

    def group_queries(h):
        heads = [h * GQA_GROUP + g for g in range(GQA_GROUP)]
        q = jnp.concatenate([q_ref[:, hq * HEAD_DIM:(hq + 1) * HEAD_DIM] for hq in heads], axis=0)
        snk = jnp.concatenate([jnp.broadcast_to(sink_ref[hq:hq + 1, 0:1], (BLOCK, 1)) for hq in heads], axis=0)
        return q, snk

    def softmax_av(h, s, snk, v):
        m = jnp.maximum(jnp.max(s, axis=-1, keepdims=True), snk)
        p = jnp.exp(s - m)
        l = jnp.sum(p, axis=-1, keepdims=True) + jnp.exp(snk - m)
        o = jnp.dot(p.astype(BF16), v, preferred_element_type=F32) / l
        for g in range(GQA_GROUP):
            sl = slice((h * GQA_GROUP + g) * HEAD_DIM, (h * GQA_GROUP + g + 1) * HEAD_DIM)
            o_ref[:, sl] = (o[g * BLOCK:(g + 1) * BLOCK] * ga_ref[:, sl].astype(F32)).astype(BF16)

    @pl.when(n < N_BLOCKS)
    def _():
        ri = lax.broadcasted_iota(jnp.int32, (grp, BLOCK), 0) & (BLOCK - 1)
        cj = lax.broadcasted_iota(jnp.int32, (grp, BLOCK), 1)
        mask_p = (cj >= ri) & (n > 0)
        mask_n = (cj <= ri) & (n < N_BLOCKS - 1)
        for h in range(N_KV_HEADS):
            ks = slice(h * HEAD_DIM, (h + 1) * HEAD_DIM)
            k = jnp.concatenate([kp_ref[:, ks], km_ref[:, ks], kn_ref[:, ks], kc_ref[:, ks]], axis=0)
            v = jnp.concatenate([vp_ref[:, ks], vm_ref[:, ks], vn_ref[:, ks], vc_ref[:, ks]], axis=0)
            q, snk = group_queries(h)
            s = _dot_nt(q, k) * scale
            s = jnp.concatenate([jnp.where(mask_p, s[:, 0:BLOCK], NEG_INF), s[:, BLOCK:2 * BLOCK],
                                 jnp.where(mask_n, s[:, 2 * BLOCK:3 * BLOCK], NEG_INF), s[:, 3 * BLOCK:]], axis=1)
            softmax_av(h, s, snk, v)

    @pl.when(n >= N_BLOCKS)
    def _():
        for h in range(N_KV_HEADS):
            ks = slice(h * HEAD_DIM, (h + 1) * HEAD_DIM)
            q, snk = group_queries(h)
            softmax_av(h, _dot_nt(q, kc_ref[:, ks]) * scale, snk, vc_ref[:, ks])


def _attention(proj, sink_b):
    rows = proj.shape[0]
    kcol = J_K * IN_TN // KV_W
    vcol = J_V * IN_TN // KV_W
    ctx_blk = SEQ // CTX_LEN
    lo = lambda n: jnp.clip(n - 1, 0, N_BLOCKS - 1)
    mid = lambda n: jnp.minimum(n, N_BLOCKS - 1)
    hi = lambda n: jnp.clip(n + 1, 0, N_BLOCKS - 1)
    kv = lambda f, col: pl.BlockSpec((BLOCK, KV_W), lambda n: (f(n), col))
    return pl.pallas_call(
        _attn_kernel,
        grid=(rows // BLOCK,),
        in_specs=[
            pl.BlockSpec((BLOCK, ATTN_W), lambda n: (n, J_Q[0] * IN_TN // ATTN_W)),
            pl.BlockSpec((BLOCK, ATTN_W), lambda n: (n, J_GA[0] * IN_TN // ATTN_W)),
            kv(lo, kcol), kv(mid, kcol), kv(hi, kcol),
            kv(lo, vcol), kv(mid, vcol), kv(hi, vcol),
            pl.BlockSpec((CTX_LEN, KV_W), lambda n: (ctx_blk, kcol)),
            pl.BlockSpec((CTX_LEN, KV_W), lambda n: (ctx_blk, vcol)),
            pl.BlockSpec((N_Q_HEADS, HEAD_DIM), lambda n: (0, 0)),
        ],
        out_specs=pl.BlockSpec((BLOCK, ATTN_W), lambda n: (n, 0)),
        out_shape=jax.ShapeDtypeStruct((rows, ATTN_W), BF16),
        compiler_params=_params(("arbitrary",), 40),
        name="band_attention",
    )(proj, proj, proj, proj, proj, proj, proj, proj, proj, proj, sink_b)


def _gmlp_kernel(u_ref, vg_ref, gb_ref, ws_ref, wsb_ref, lg_ref, lb_ref, o_ref):
    v = vg_ref[...].astype(F32)
    mu = jnp.mean(v, axis=-1, keepdims=True)
    vc = v - mu
    var = jnp.mean(vc * vc, axis=-1, keepdims=True)
    y = (vc * lax.rsqrt(var + LN_EPS) * lg_ref[...] + lb_ref[...]).astype(BF16)
    for g in range(GMLP_GROUPS):
        sl = slice(g * HEAD_DIM, (g + 1) * HEAD_DIM)
        s = jnp.dot(ws_ref[g].astype(BF16), y[:, sl], preferred_element_type=F32) + wsb_ref[g]
        o_ref[:, sl] = (u_ref[:, sl].astype(F32) * s * gb_ref[:, sl].astype(F32)).astype(BF16)


def _gmlp(proj, ws, wsb_b, ln_g, ln_b, layer_i):
    rows = proj.shape[0]
    col = lambda c: pl.BlockSpec((BLOCK, GMLP_W), lambda n: (n, c))
    vec = pl.BlockSpec((None, 1, GMLP_W), lambda n: (layer_i, 0, 0))
    return pl.pallas_call(
        _gmlp_kernel,
        grid=(rows // BLOCK,),
        in_specs=[
            col(J_U[0] * IN_TN // GMLP_W), col(J_VG[0] * IN_TN // GMLP_W), col(J_GB[0] * IN_TN // GMLP_W),
            pl.BlockSpec((None, GMLP_GROUPS, BLOCK, BLOCK), lambda n: (layer_i, 0, 0, 0)),
            pl.BlockSpec((GMLP_GROUPS, BLOCK, BLOCK), lambda n: (0, 0, 0)),
            vec, vec,
        ],
        out_specs=pl.BlockSpec((BLOCK, GMLP_W), lambda n: (n, 0)),
        out_shape=jax.ShapeDtypeStruct((rows, GMLP_W), BF16),
        compiler_params=_params(("arbitrary",), 40),
        name="chunk_gmlp",
    )(proj, proj, proj, ws, wsb_b, ln_g, ln_b)


def _outproj_kernel(a1_ref, a2_ref, w_ref, y_ref, wb_ref):
    _cast_weight_once(w_ref, wb_ref)
    half = D_MODEL // 2
    for hf in range(y_ref.shape[1] // MXU_N):
        cs = slice(hf * MXU_N, (hf + 1) * MXU_N)
        acc = (jnp.dot(a1_ref[...], wb_ref[0:half, cs], preferred_element_type=F32)
               + jnp.dot(a2_ref[...], wb_ref[half:D_MODEL, cs], preferred_element_type=F32))
        y_ref[:, cs] = acc.astype(BF16)


def _outproj(a1, a2, a2_col, w, layer_i, rows, tm):
    tn = 512
    half = D_MODEL // 2
    return pl.pallas_call(
        _outproj_kernel,
        grid=(D_MODEL // tn, rows // tm),
        in_specs=[
            pl.BlockSpec((tm, half), lambda j, i: (i, 0)),
            pl.BlockSpec((tm, half), lambda j, i: (i, a2_col)),
            pl.BlockSpec((None, D_MODEL, tn), lambda j, i: (layer_i, 0, j)),
        ],
        out_specs=pl.BlockSpec((tm, tn), lambda j, i: (i, j)),
        out_shape=jax.ShapeDtypeStruct((rows, D_MODEL), BF16),
        scratch_shapes=[pltpu.VMEM((D_MODEL, tn), BF16)],
        compiler_params=_params(("arbitrary", "arbitrary"), 52),
        name="outproj",
    )(a1, a2, w)


def _inproj_conv_kernel(h_ref, wa_ref, wb_ref, wg_ref, glu_ref, sg_ref, wab_ref, wbb_ref, wgb_ref):
    _cast_weight_once(wa_ref, wab_ref)
    _cast_weight_once(wb_ref, wbb_ref)
    _cast_weight_once(wg_ref, wgb_ref)
    h = h_ref[...]
    a = jnp.dot(h, wab_ref[...], preferred_element_type=F32)
    b = jnp.dot(h, wbb_ref[...], preferred_element_type=F32)
    glu_ref[...] = (a * jax.nn.sigmoid(b)).astype(BF16)
    g = jnp.dot(h, wgb_ref[...], preferred_element_type=F32)
    sg_ref[...] = _silu(g).astype(BF16)


def _inproj_conv(h, w, layer_i, tm):
    rows = h.shape[0]
    tn = MXU_N
    nj = D_MODEL // tn
    wspec = lambda part: pl.BlockSpec((None, D_MODEL, tn), lambda j, i: (layer_i, 0, j + part * nj))
    out = pl.BlockSpec((tm, tn), lambda j, i: (i, j))
    return pl.pallas_call(
        _inproj_conv_kernel,
        grid=(nj, rows // tm),
        in_specs=[pl.BlockSpec((tm, D_MODEL), lambda j, i: (i, 0)), wspec(0), wspec(1), wspec(2)],
        out_specs=[out, out],
        out_shape=[jax.ShapeDtypeStruct((rows, D_MODEL), BF16)] * 2,
        scratch_shapes=[pltpu.VMEM((D_MODEL, tn), BF16)] * 3,
        compiler_params=_params(("arbitrary", "arbitrary"), 52),
        name="inproj_conv",
    )(h, w, w, w)


def _conv_kernel(gp_ref, gm_ref, gn_ref, sg_ref, dw_ref, dwb_ref, lg_ref, lb_ref, o_ref, y_ref, s1_ref):
    i = pl.program_id(0)
    tr = ROW_TILE
    first_ctx = SEQ // tr
    has_prev = (i != 0) & (i != first_ctx)
    has_next = (i != first_ctx - 1) & (i != pl.num_programs(0) - 1)
    base = CONV_HALO - CONV_K // 2
    sub = 128
    span = sub + 2 * CONV_HALO - SUBLANES
    s1_ref[...] = jnp.zeros_like(s1_ref)

    def lane_block(c, carry):
        ls = pl.ds(pl.multiple_of(c * LANES, LANES), LANES)
        prev = jnp.where(has_prev, gp_ref[:, ls].astype(F32), 0.0)
        nxt = jnp.where(has_next, gn_ref[:, ls].astype(F32), 0.0)
        window = jnp.concatenate([prev, gm_ref[:, ls].astype(F32), nxt], axis=0)
        for r0 in range(0, tr, sub):
            win = window[r0:r0 + sub + 2 * CONV_HALO]
            acc = jnp.broadcast_to(dwb_ref[:, ls], (sub, LANES))
            for s in range(SUBLANES):
                shifted = win if s == 0 else pltpu.roll(win, win.shape[0] - s, axis=0)
                for q in range(span // SUBLANES):
                    d = q * SUBLANES + s - base
                    if 0 <= d < CONV_K and q * SUBLANES + sub <= span:
                        acc = acc + shifted[q * SUBLANES:q * SUBLANES + sub] * dw_ref[d:d + 1, ls]
            y_ref[r0:r0 + sub, ls] = acc
            s1_ref[r0:r0 + sub, :] += acc
        return carry

    lax.fori_loop(0, D_MODEL // LANES, lane_block, 0)
    mu = jnp.sum(s1_ref[...], axis=-1, keepdims=True) * np.float32(1.0 / D_MODEL)
    yc = y_ref[...] - mu
    var = jnp.mean(yc * yc, axis=-1, keepdims=True)
    z = yc * lax.rsqrt(var + LN_EPS) * lg_ref[...] + lb_ref[...]
    o_ref[...] = (_silu(z) * sg_ref[...].astype(F32)).astype(BF16)


def _conv_mix(glu, sg, dw, dw_b, ln_g, ln_b, layer_i, rows):
    tr = ROW_TILE
    per = tr // CONV_HALO
    n_halo = glu.shape[0] // CONV_HALO
    vec = pl.BlockSpec((None, 1, D_MODEL), lambda i: (layer_i, 0, 0))
    return pl.pallas_call(
        _conv_kernel,
        grid=(rows // tr,),
        in_specs=[
            pl.BlockSpec((CONV_HALO, D_MODEL), lambda i: (jnp.maximum(i * per - 1, 0), 0)),
            pl.BlockSpec((tr, D_MODEL), lambda i: (i, 0)),
            pl.BlockSpec((CONV_HALO, D_MODEL), lambda i: (jnp.minimum((i + 1) * per, n_halo - 1), 0)),
            pl.BlockSpec((tr, D_MODEL), lambda i: (i, 0)),
            pl.BlockSpec((None, CONV_K, D_MODEL), lambda i: (layer_i, 0, 0)),
            vec, vec, vec,
        ],
        out_specs=pl.BlockSpec((tr, D_MODEL), lambda i: (i, 0)),
        out_shape=jax.ShapeDtypeStruct((rows, D_MODEL), BF16),
        scratch_shapes=[pltpu.VMEM((tr, D_MODEL), F32), pltpu.VMEM((tr, LANES), F32)],
        compiler_params=_params(("arbitrary",), 48),
        name="conv_ln_swish",
    )(glu, glu, glu, sg, dw, dw_b, ln_g, ln_b)


def _rope_tables():
    quarter = HEAD_DIM // 4
    inv = 1.0 / (ROPE_BASE ** (jnp.arange(quarter, dtype=F32) / quarter))
    pos = jnp.arange(SEQ, dtype=jnp.int32)
    row = (pos // GRID_W).astype(F32)[:, None] * inv[None, :]
    col = (pos % GRID_W).astype(F32)[:, None] * inv[None, :]
    zero = jnp.zeros_like(row)
    cos = jnp.concatenate([jnp.cos(row), jnp.cos(row), jnp.cos(col), jnp.cos(col)], axis=-1)
    s_up = jnp.concatenate([-jnp.sin(row), zero, -jnp.sin(col), zero], axis=-1)
    s_dn = jnp.concatenate([zero, jnp.sin(row), zero, jnp.sin(col)], axis=-1)
    pad = lambda t, v: jnp.concatenate([t, jnp.full((CTX_LEN, HEAD_DIM), v, F32)], axis=0)
    return pad(cos, 1.0), pad(s_up, 0.0), pad(s_dn, 0.0)


def kernel(x, c, ctx, c_ctx, ada_w, ada_b, pre_g, post_g, ab_w_in, ab_sink, ab_ln_g, ab_ln_b, ab_ws, ab_ws_b,
           ab_w_out, cv_w_in, cv_dw, cv_dw_b, cv_ln_g, cv_ln_b, cv_w_out):
    assert x.shape == (1, SEQ, D_MODEL) and ctx.shape == (1, CTX_LEN, D_MODEL)
    cc = jnp.concatenate([c, c_ctx[None, :], jnp.zeros((6, D_MODEL), F32)], axis=0)
    mod = _modulation(cc, ada_w, ada_b)
    rope_c, rope_s1, rope_s2 = _rope_tables()
    pre_g = pre_g.reshape(DEPTH, 1, D_MODEL)
    post_g = post_g.reshape(DEPTH, 1, D_MODEL)
    vec3 = lambda t: t.reshape(t.shape[0], 1, t.shape[1])

    x_all, h = _first_prenorm(x, ctx, mod, pre_g)
    y = None
    for layer in range(DEPTH):
        last = layer == DEPTH - 1
        i = layer // 2
        rows = SEQ if last else ROWS
        tm = TM_LATENT if last else TM_ALL
        if layer > 0:
            x_all, h = _post_pre(x_all, y, mod, post_g, pre_g, layer, rows)
        if layer % 2 == 0:
            proj = _inproj_even(h, ab_w_in, i, rope_c, rope_s1, rope_s2, tm)
            sink_b = jnp.broadcast_to(ab_sink[i][:, None], (N_Q_HEADS, HEAD_DIM))
            attn = _attention(proj, sink_b)
            wsb_b = jnp.broadcast_to(ab_ws_b[i][:, :, None], (GMLP_GROUPS, BLOCK, BLOCK))
            mix = _gmlp(proj, ab_ws, wsb_b, vec3(ab_ln_g), vec3(ab_ln_b), i)
            y = _outproj(attn, mix, 0, ab_w_out, i, rows, tm)
        else:
            glu, sg = _inproj_conv(h, cv_w_in, i, tm)
            z = _conv_mix(glu, sg, cv_dw, vec3(cv_dw_b), vec3(cv_ln_g), vec3(cv_ln_b), i, rows)
            y = _outproj(z, z, 1, cv_w_out, i, rows, tm)
    out = _final_post(x_all, y, mod, post_g, DEPTH - 1, SEQ)
    return out[None]
```

```python
import functools

import jax
import jax.numpy as jnp
import numpy as np
from jax import lax
from jax.experimental import pallas as pl
from jax.experimental.pallas import tpu as pltpu

D_MODEL = 4096
SEQ = 8192
DEPTH = 4
GRID_W = 64
CTX_LEN = 256
ROWS = SEQ + CTX_LEN
HEAD_DIM = 128
N_Q_HEADS = 16
N_KV_HEADS = 4
GQA_GROUP = 4
ATTN_W = 2048
KV_W = 512
BLOCK = 128
N_BLOCKS = SEQ // BLOCK
GMLP_GROUPS = 16
GMLP_W = 2048
AB_IN_W = 11264
CONV_K = 31
CONV_HALO = 16
ROPE_BASE = 10000.0
RMS_EPS = 1e-6
LN_EPS = 1e-5
NEG_INF = -1e30

F32 = jnp.float32
BF16 = jnp.bfloat16
MIB = 1024 * 1024
LANES = 128
SUBLANES = 8
MXU_N = 256

ROW_TILE = 256
CAST_ROWS = 256
TM_ALL = 1056
TM_LATENT = 1024
NORM_ROWS = 16

IN_TN = 512
J_Q = (0, 4)
J_GA = (4, 8)
J_U = (8, 12)
J_VG = (12, 16)
J_GB = (16, 20)
J_K = 20
J_V = 21


def _params(semantics, vmem_mib):
    return pltpu.CompilerParams(dimension_semantics=semantics, vmem_limit_bytes=vmem_mib * MIB)


def _silu(t):
    return t * jax.nn.sigmoid(t)


def _gelu(t):
    return 0.5 * t * (1.0 + lax.erf(t * np.float32(np.sqrt(0.5))))


def _rms(t):
    return t * lax.rsqrt(jnp.mean(t * t, axis=-1, keepdims=True) + RMS_EPS)


def _cast_weight_once(w_ref, wb_ref):
    @pl.when(pl.program_id(1) == 0)
    def _():
        def rows(r, carry):
            rs = pl.ds(pl.multiple_of(r * CAST_ROWS, CAST_ROWS), CAST_ROWS)
            wb_ref[rs, :] = w_ref[rs, :].astype(BF16)
            return carry
        lax.fori_loop(0, w_ref.shape[0] // CAST_ROWS, rows, 0)


def _mod_kernel(cc_ref, w_ref, b_ref, o_ref):
    a = _silu(cc_ref[...]).astype(BF16)
    w = w_ref[...].astype(BF16)
    o_ref[...] = jnp.dot(a, w, preferred_element_type=F32) + b_ref[...]


def _modulation(cc, ada_w, ada_b):
    tn = 512
    n3 = 3 * D_MODEL
    return pl.pallas_call(
        _mod_kernel,
        grid=(DEPTH, n3 // tn),
        in_specs=[
            pl.BlockSpec((8, D_MODEL), lambda l, j: (0, 0)),
            pl.BlockSpec((None, D_MODEL, tn), lambda l, j: (l, 0, j)),
            pl.BlockSpec((None, 1, tn), lambda l, j: (l, 0, j)),
        ],
        out_specs=pl.BlockSpec((None, 8, tn), lambda l, j: (l, 0, j)),
        out_shape=jax.ShapeDtypeStruct((DEPTH, 8, n3), F32),
        compiler_params=_params(("arbitrary", "arbitrary"), 40),
        name="adaln_mod",
    )(cc, ada_w, ada_b.reshape(DEPTH, 1, n3))


def _modulate(x, mod_ref, g_ref, is_ctx):
    shift = jnp.where(is_ctx, mod_ref[1:2, 0:D_MODEL], mod_ref[0:1, 0:D_MODEL])
    scale = jnp.where(is_ctx, mod_ref[1:2, D_MODEL:2 * D_MODEL], mod_ref[0:1, D_MODEL:2 * D_MODEL])
    return (_rms(x) * g_ref[...] * (1.0 + scale) + shift).astype(BF16)


def _residual(x, y_ref, mod_ref, pg_ref, is_ctx):
    gate = jnp.where(is_ctx, mod_ref[1:2, 2 * D_MODEL:3 * D_MODEL], mod_ref[0:1, 2 * D_MODEL:3 * D_MODEL])
    return x + gate * (_rms(y_ref[...].astype(F32)) * pg_ref[...])


def _first_prenorm_kernel(x_ref, ctx_ref, mod_ref, g_ref, h_ref):
    is_ctx = pl.program_id(0) * ROW_TILE >= SEQ
    x = jnp.where(is_ctx, ctx_ref[...], x_ref[...])
    h_ref[...] = _modulate(x, mod_ref, g_ref, is_ctx)


def _post_pre_kernel(x_ref, y_ref, modp_ref, pg_ref, mod_ref, g_ref, xo_ref, h_ref):
    is_ctx = pl.program_id(0) * ROW_TILE >= SEQ
    x = _residual(x_ref[...], y_ref, modp_ref, pg_ref, is_ctx)
    xo_ref[...] = x
    h_ref[...] = _modulate(x, mod_ref, g_ref, is_ctx)


def _post_pre_split_kernel(x_ref, ctx_ref, y_ref, modp_ref, pg_ref, mod_ref, g_ref, xo_ref, h_ref):
    is_ctx = pl.program_id(0) * ROW_TILE >= SEQ
    x = _residual(jnp.where(is_ctx, ctx_ref[...], x_ref[...]), y_ref, modp_ref, pg_ref, is_ctx)
    xo_ref[...] = x
    h_ref[...] = _modulate(x, mod_ref, g_ref, is_ctx)


def _final_post_kernel(x_ref, y_ref, modp_ref, pg_ref, xo_ref):
    xo_ref[...] = _residual(x_ref[...], y_ref, modp_ref, pg_ref, False)


def _row_spec():
    return pl.BlockSpec((ROW_TILE, D_MODEL), lambda i: (i, 0))


def _mod_spec(layer):
    return pl.BlockSpec((None, 8, 3 * D_MODEL), lambda i: (layer, 0, 0))


def _gain_spec(layer):
    return pl.BlockSpec((None, 1, D_MODEL), lambda i: (layer, 0, 0))


def _split_input_specs():
    last_x = SEQ // ROW_TILE - 1
    return [pl.BlockSpec((None, ROW_TILE, D_MODEL), lambda i: (0, jnp.minimum(i, last_x), 0)),
            pl.BlockSpec((None, ROW_TILE, D_MODEL), lambda i: (0, jnp.maximum(i - last_x - 1, 0), 0))]


def _first_prenorm(x, ctx, mod, pre_g):
    return pl.pallas_call(
        _first_prenorm_kernel,
        grid=(ROWS // ROW_TILE,),
        in_specs=_split_input_specs() + [_mod_spec(0), _gain_spec(0)],
        out_specs=_row_spec(),
        out_shape=jax.ShapeDtypeStruct((ROWS, D_MODEL), BF16),
        compiler_params=_params(("arbitrary",), 48),
        name="first_prenorm",
    )(x, ctx, mod, pre_g)


def _post_pre(xs, y, mod, post_g, pre_g, layer, rows):
    split = len(xs) == 2
    return pl.pallas_call(
        _post_pre_split_kernel if split else _post_pre_kernel,
        grid=(rows // ROW_TILE,),
        in_specs=(_split_input_specs() if split else [_row_spec()])
        + [_row_spec(), _mod_spec(layer - 1), _gain_spec(layer - 1), _mod_spec(layer), _gain_spec(layer)],
        out_specs=[_row_spec(), _row_spec()],
        out_shape=[jax.ShapeDtypeStruct((rows, D_MODEL), F32), jax.ShapeDtypeStruct((rows, D_MODEL), BF16)],
        compiler_params=_params(("arbitrary",), 48),
        name="post_pre_norm",
    )(*xs, y, mod, post_g, mod, pre_g)


def _final_post(x_all, y, mod, post_g, layer, rows):
    return pl.pallas_call(
        _final_post_kernel,
        grid=(rows // ROW_TILE,),
        in_specs=[_row_spec(), _row_spec(), _mod_spec(layer), _gain_spec(layer)],
        out_specs=_row_spec(),
        out_shape=jax.ShapeDtypeStruct((rows, D_MODEL), F32),
        compiler_params=_params(("arbitrary",), 48),
        name="final_post_norm",
    )(x_all, y, mod, post_g)


def _inproj_even_kernel(h_ref, w_ref, c_ref, s1_ref, s2_ref, o_ref, wb_ref):
    j = pl.program_id(0)
    _cast_weight_once(w_ref, wb_ref)

    def column_halves(epilogue):
        for hf in range(IN_TN // MXU_N):
            cs = slice(hf * MXU_N, (hf + 1) * MXU_N)
            acc = jnp.dot(h_ref[...], wb_ref[:, cs], preferred_element_type=F32)
            epilogue(acc, hf * MXU_N)

    def rope(acc, c0):
        c = c_ref[...]
        s1 = s1_ref[...]
        s2 = s2_ref[...]
        for hh in range(MXU_N // HEAD_DIM):
            xs = acc[:, hh * HEAD_DIM:(hh + 1) * HEAD_DIM]
            up = pltpu.roll(xs, HEAD_DIM - 32, axis=1)
            dn = pltpu.roll(xs, 32, axis=1)
            o_ref[:, c0 + hh * HEAD_DIM:c0 + (hh + 1) * HEAD_DIM] = (xs * c + up * s1 + dn * s2).astype(BF16)

    def store(fn):
        def epilogue(acc, c0):
            o_ref[:, c0:c0 + MXU_N] = fn(acc).astype(BF16)
        return epilogue

    @pl.when((j < J_Q[1]) | (j == J_K))
    def _():
        column_halves(rope)

    @pl.when(j == J_V)
    def _():
        column_halves(store(lambda t: t))

    @pl.when(((j >= J_GA[0]) & (j < J_GA[1])) | ((j >= J_GB[0]) & (j < J_GB[1])))
    def _():
        column_halves(store(_silu))

    @pl.when((j >= J_U[0]) & (j < J_VG[1]))
    def _():
        column_halves(store(_gelu))


def _even_weight_block(j):
    return jnp.where(j < J_Q[1], j, jnp.where(j < J_K, j + 2, j - (J_K - J_Q[1])))


def _inproj_even(h, w_in, layer_i, rope_c, rope_s1, rope_s2, tm):
    rows = h.shape[0]
    tbl = pl.BlockSpec((tm, HEAD_DIM), lambda j, i: (i, 0))
    return pl.pallas_call(
        _inproj_even_kernel,
        grid=(AB_IN_W // IN_TN, rows // tm),
        in_specs=[
            pl.BlockSpec((tm, D_MODEL), lambda j, i: (i, 0)),
            pl.BlockSpec((None, D_MODEL, IN_TN), lambda j, i: (layer_i, 0, _even_weight_block(j))),
            tbl, tbl, tbl,
        ],
        out_specs=pl.BlockSpec((tm, IN_TN), lambda j, i: (i, j)),
        out_shape=jax.ShapeDtypeStruct((rows, AB_IN_W), BF16),
        scratch_shapes=[pltpu.VMEM((D_MODEL, IN_TN), BF16)],
        compiler_params=_params(("arbitrary", "arbitrary"), 58),
        name="inproj_even",
    )(h, w_in, rope_c, rope_s1, rope_s2)


def _dot_nt(a, b):
    return lax.dot_general(a, b, (((1,), (1,)), ((), ())), preferred_element_type=F32)


def _attn_kernel(q_ref, ga_ref, kp_ref, km_ref, kn_ref, vp_ref, vm_ref, vn_ref, kc_ref, vc_ref,
                 sink_ref, o_ref):
    n = pl.program_id(0)
    scale = np.float32(HEAD_DIM ** -0.5)
    grp = GQA_GROUP * BLOCK

    def group_queries(h):
        heads = [h * GQA_GROUP + g for g in range(GQA_GROUP)]
        q = jnp.concatenate([q_ref[:, hq * HEAD_DIM:(hq + 1) * HEAD_DIM] for hq in heads], axis=0)
        snk = jnp.concatenate([jnp.broadcast_to(sink_ref[hq:hq + 1, 0:1], (BLOCK, 1)) for hq in heads], axis=0)
        return q, snk

    def softmax_av(h, s, snk, v):
        m = jnp.maximum(jnp.max(s, axis=-1, keepdims=True), snk)
        p = jnp.exp(s - m)
        l = jnp.sum(p, axis=-1, keepdims=True) + jnp.exp(snk - m)
        o = jnp.dot(p.astype(BF16), v, preferred_element_type=F32) / l
        for g in range(GQA_GROUP):
            sl = slice((h * GQA_GROUP + g) * HEAD_DIM, (h * GQA_GROUP + g + 1) * HEAD_DIM)
            o_ref[:, sl] = (o[g * BLOCK:(g + 1) * BLOCK] * ga_ref[:, sl].astype(F32)).astype(BF16)

    @pl.when(n < N_BLOCKS)
    def _():
        ri = lax.broadcasted_iota(jnp.int32, (grp, BLOCK), 0) & (BLOCK - 1)
        cj = lax.broadcasted_iota(jnp.int32, (grp, BLOCK), 1)
        mask_p = (cj >= ri) & (n > 0)
        mask_n = (cj <= ri) & (n < N_BLOCKS - 1)
        for h in range(N_KV_HEADS):
            ks = slice(h * HEAD_DIM, (h + 1) * HEAD_DIM)
            k = jnp.concatenate([kp_ref[:, ks], km_ref[:, ks], kn_ref[:, ks], kc_ref[:, ks]], axis=0)
            v = jnp.concatenate([vp_ref[:, ks], vm_ref[:, ks], vn_ref[:, ks], vc_ref[:, ks]], axis=0)
            q, snk = group_queries(h)
            s = _dot_nt(q, k) * scale
            s = jnp.concatenate([jnp.where(mask_p, s[:, 0:BLOCK], NEG_INF), s[:, BLOCK:2 * BLOCK],
                                 jnp.where(mask_n, s[:, 2 * BLOCK:3 * BLOCK], NEG_INF), s[:, 3 * BLOCK:]], axis=1)
            softmax_av(h, s, snk, v)

    @pl.when(n >= N_BLOCKS)
    def _():
        for h in range(N_KV_HEADS):
            ks = slice(h * HEAD_DIM, (h + 1) * HEAD_DIM)
            q, snk = group_queries(h)
            softmax_av(h, _dot_nt(q, kc_ref[:, ks]) * scale, snk, vc_ref[:, ks])


def _attention(proj, sink_b):
    rows = proj.shape[0]
    kcol = J_K * IN_TN // KV_W
    vcol = J_V * IN_TN // KV_W
    ctx_blk = SEQ // CTX_LEN
    lo = lambda n: jnp.clip(n - 1, 0, N_BLOCKS - 1)
    mid = lambda n: jnp.minimum(n, N_BLOCKS - 1)
    hi = lambda n: jnp.clip(n + 1, 0, N_BLOCKS - 1)
    kv = lambda f, col: pl.BlockSpec((BLOCK, KV_W), lambda n: (f(n), col))
    return pl.pallas_call(
        _attn_kernel,
        grid=(rows // BLOCK,),
        in_specs=[
            pl.BlockSpec((BLOCK, ATTN_W), lambda n: (n, J_Q[0] * IN_TN // ATTN_W)),
            pl.BlockSpec((BLOCK, ATTN_W), lambda n: (n, J_GA[0] * IN_TN // ATTN_W)),
            kv(lo, kcol), kv(mid, kcol), kv(hi, kcol),
            kv(lo, vcol), kv(mid, vcol), kv(hi, vcol),
            pl.BlockSpec((CTX_LEN, KV_W), lambda n: (ctx_blk, kcol)),
            pl.BlockSpec((CTX_LEN, KV_W), lambda n: (ctx_blk, vcol)),
            pl.BlockSpec((N_Q_HEADS, HEAD_DIM), lambda n: (0, 0)),
        ],
        out_specs=pl.BlockSpec((BLOCK, ATTN_W), lambda n: (n, 0)),
        out_shape=jax.ShapeDtypeStruct((rows, ATTN_W), BF16),
        compiler_params=_params(("arbitrary",), 40),
        name="band_attention",
    )(proj, proj, proj, proj, proj, proj, proj, proj, proj, proj, sink_b)


def _gmlp_kernel(u_ref, vg_ref, gb_ref, ws_ref, wsb_ref, lg_ref, lb_ref, o_ref):
    for ch in range(ROW_TILE // BLOCK):
        rs = slice(ch * BLOCK, (ch + 1) * BLOCK)
        v = vg_ref[rs, :].astype(F32)
        mu = jnp.mean(v, axis=-1, keepdims=True)
        vc = v - mu
        var = jnp.mean(vc * vc, axis=-1, keepdims=True)
        y = (vc * lax.rsqrt(var + LN_EPS) * lg_ref[...] + lb_ref[...]).astype(BF16)
        for g in range(GMLP_GROUPS):
            sl = slice(g * HEAD_DIM, (g + 1) * HEAD_DIM)
            s = jnp.dot(ws_ref[g].astype(BF16), y[:, sl], preferred_element_type=F32) + wsb_ref[g]
            o_ref[rs, sl] = (u_ref[rs, sl].astype(F32) * s * gb_ref[rs, sl].astype(F32)).astype(BF16)


def _gmlp(proj, ws, wsb_b, ln_g, ln_b, layer_i):
    rows = proj.shape[0]
    col = lambda c: pl.BlockSpec((ROW_TILE, GMLP_W), lambda n: (n, c))
    vec = pl.BlockSpec((None, 1, GMLP_W), lambda n: (layer_i, 0, 0))
    return pl.pallas_call(
        _gmlp_kernel,
        grid=(rows // ROW_TILE,),
        in_specs=[
            col(J_U[0] * IN_TN // GMLP_W), col(J_VG[0] * IN_TN // GMLP_W), col(J_GB[0] * IN_TN // GMLP_W),
            pl.BlockSpec((None, GMLP_GROUPS, BLOCK, BLOCK), lambda n: (layer_i, 0, 0, 0)),
            pl.BlockSpec((GMLP_GROUPS, BLOCK, BLOCK), lambda n: (0, 0, 0)),
            vec, vec,
        ],
        out_specs=pl.BlockSpec((ROW_TILE, GMLP_W), lambda n: (n, 0)),
        out_shape=jax.ShapeDtypeStruct((rows, GMLP_W), BF16),
        compiler_params=_params(("arbitrary",), 40),
        name="chunk_gmlp",
    )(proj, proj, proj, ws, wsb_b, ln_g, ln_b)


def _outproj_kernel(a1_ref, a2_ref, w_ref, y_ref, wb_ref):
    _cast_weight_once(w_ref, wb_ref)
    half = D_MODEL // 2
    for hf in range(y_ref.shape[1] // MXU_N):
        cs = slice(hf * MXU_N, (hf + 1) * MXU_N)
        acc = (jnp.dot(a1_ref[...], wb_ref[0:half, cs], preferred_element_type=F32)
               + jnp.dot(a2_ref[...], wb_ref[half:D_MODEL, cs], preferred_element_type=F32))
        y_ref[:, cs] = acc.astype(BF16)


def _outproj(a1, a2, a2_col, w, layer_i, rows, tm):
    tn = 512
    half = D_MODEL // 2
    return pl.pallas_call(
        _outproj_kernel,
        grid=(D_MODEL // tn, rows // tm),
        in_specs=[
            pl.BlockSpec((tm, half), lambda j, i: (i, 0)),
            pl.BlockSpec((tm, half), lambda j, i: (i, a2_col)),
            pl.BlockSpec((None, D_MODEL, tn), lambda j, i: (layer_i, 0, j)),
        ],
        out_specs=pl.BlockSpec((tm, tn), lambda j, i: (i, j)),
        out_shape=jax.ShapeDtypeStruct((rows, D_MODEL), BF16),
        scratch_shapes=[pltpu.VMEM((D_MODEL, tn), BF16)],
        compiler_params=_params(("arbitrary", "arbitrary"), 52),
        name="outproj",
    )(a1, a2, w)


def _inproj_conv_kernel(h_ref, wa_ref, wb_ref, wg_ref, glu_ref, sg_ref, wab_ref, wbb_ref, wgb_ref):
    _cast_weight_once(wa_ref, wab_ref)
    _cast_weight_once(wb_ref, wbb_ref)
    _cast_weight_once(wg_ref, wgb_ref)
    h = h_ref[...]
    a = jnp.dot(h, wab_ref[...], preferred_element_type=F32)
    b = jnp.dot(h, wbb_ref[...], preferred_element_type=F32)
    glu_ref[...] = (a * jax.nn.sigmoid(b)).astype(BF16)
    g = jnp.dot(h, wgb_ref[...], preferred_element_type=F32)
    sg_ref[...] = _silu(g).astype(BF16)


def _inproj_conv(h, w, layer_i, tm):
    rows = h.shape[0]
    tn = MXU_N
    nj = D_MODEL // tn
    wspec = lambda part: pl.BlockSpec((None, D_MODEL, tn), lambda j, i: (layer_i, 0, j + part * nj))
    out = pl.BlockSpec((tm, tn), lambda j, i: (i, j))
    return pl.pallas_call(
        _inproj_conv_kernel,
        grid=(nj, rows // tm),
        in_specs=[pl.BlockSpec((tm, D_MODEL), lambda j, i: (i, 0)), wspec(0), wspec(1), wspec(2)],
        out_specs=[out, out],
        out_shape=[jax.ShapeDtypeStruct((rows, D_MODEL), BF16)] * 2,
        scratch_shapes=[pltpu.VMEM((D_MODEL, tn), BF16)] * 3,
        compiler_params=_params(("arbitrary", "arbitrary"), 60),
        name="inproj_conv",
    )(h, w, w, w)


def _conv_kernel(gp_ref, gm_ref, gn_ref, sg_ref, dw_ref, dwb_ref, lg_ref, lb_ref, o_ref, y_ref, s1_ref):
    i = pl.program_id(0)
    tr = ROW_TILE
    first_ctx = SEQ // tr
    has_prev = (i != 0) & (i != first_ctx)
    has_next = (i != first_ctx - 1) & (i != pl.num_programs(0) - 1)
    base = CONV_HALO - CONV_K // 2
    sub = 128
    span = sub + 2 * CONV_HALO - SUBLANES
    s1_ref[...] = jnp.zeros_like(s1_ref)

    def lane_block(c, carry):
        ls = pl.ds(pl.multiple_of(c * LANES, LANES), LANES)
        prev = jnp.where(has_prev, gp_ref[:, ls].astype(F32), 0.0)
        nxt = jnp.where(has_next, gn_ref[:, ls].astype(F32), 0.0)
        window = jnp.concatenate([prev, gm_ref[:, ls].astype(F32), nxt], axis=0)
        for r0 in range(0, tr, sub):
            win = window[r0:r0 + sub + 2 * CONV_HALO]
            acc = jnp.broadcast_to(dwb_ref[:, ls], (sub, LANES))
            for s in range(SUBLANES):
                shifted = win if s == 0 else pltpu.roll(win, win.shape[0] - s, axis=0)
                for q in range(span // SUBLANES):
                    d = q * SUBLANES + s - base
                    if 0 <= d < CONV_K and q * SUBLANES + sub <= span:
                        acc = acc + shifted[q * SUBLANES:q * SUBLANES + sub] * dw_ref[d:d + 1, ls]
            y_ref[r0:r0 + sub, ls] = acc
            s1_ref[r0:r0 + sub, :] += acc
        return carry

    lax.fori_loop(0, D_MODEL // LANES, lane_block, 0)

    def norm_rows(r, carry):
        rs = pl.ds(pl.multiple_of(r * NORM_ROWS, NORM_ROWS), NORM_ROWS)
        mu = jnp.sum(s1_ref[rs, :], axis=-1, keepdims=True) * np.float32(1.0 / D_MODEL)
        yc = y_ref[rs, :] - mu
        var = jnp.mean(yc * yc, axis=-1, keepdims=True)
        z = yc * lax.rsqrt(var + LN_EPS) * lg_ref[...] + lb_ref[...]
        o_ref[rs, :] = (_silu(z) * sg_ref[rs, :].astype(F32)).astype(BF16)
        return carry

    lax.fori_loop(0, tr // NORM_ROWS, norm_rows, 0, unroll=4)


def _conv_mix(glu, sg, dw, dw_b, ln_g, ln_b, layer_i, rows):
    tr = ROW_TILE
    per = tr // CONV_HALO
    n_halo = glu.shape[0] // CONV_HALO
    vec = pl.BlockSpec((None, 1, D_MODEL), lambda i: (layer_i, 0, 0))
    return pl.pallas_call(
        _conv_kernel,
        grid=(rows // tr,),
        in_specs=[
            pl.BlockSpec((CONV_HALO, D_MODEL), lambda i: (jnp.maximum(i * per - 1, 0), 0)),
            pl.BlockSpec((tr, D_MODEL), lambda i: (i, 0)),
            pl.BlockSpec((CONV_HALO, D_MODEL), lambda i: (jnp.minimum((i + 1) * per, n_halo - 1), 0)),
            pl.BlockSpec((tr, D_MODEL), lambda i: (i, 0)),
            pl.BlockSpec((None, CONV_K, D_MODEL), lambda i: (layer_i, 0, 0)),
            vec, vec, vec,
        ],
        out_specs=pl.BlockSpec((tr, D_MODEL), lambda i: (i, 0)),
        out_shape=jax.ShapeDtypeStruct((rows, D_MODEL), BF16),
        scratch_shapes=[pltpu.VMEM((tr, D_MODEL), F32), pltpu.VMEM((tr, LANES), F32)],
        compiler_params=_params(("arbitrary",), 48),
        name="conv_ln_swish",
    )(glu, glu, glu, sg, dw, dw_b, ln_g, ln_b)


def _rope_tables():
    quarter = HEAD_DIM // 4
    inv = 1.0 / (ROPE_BASE ** (jnp.arange(quarter, dtype=F32) / quarter))
    pos = jnp.arange(SEQ, dtype=jnp.int32)
    row = (pos // GRID_W).astype(F32)[:, None] * inv[None, :]
    col = (pos % GRID_W).astype(F32)[:, None] * inv[None, :]
    zero = jnp.zeros_like(row)
    cos = jnp.concatenate([jnp.cos(row), jnp.cos(row), jnp.cos(col), jnp.cos(col)], axis=-1)
    s_up = jnp.concatenate([-jnp.sin(row), zero, -jnp.sin(col), zero], axis=-1)
    s_dn = jnp.concatenate([zero, jnp.sin(row), zero, jnp.sin(col)], axis=-1)
    pad = lambda t, v: jnp.concatenate([t, jnp.full((CTX_LEN, HEAD_DIM), v, F32)], axis=0)
    return pad(cos, 1.0), pad(s_up, 0.0), pad(s_dn, 0.0)


def kernel(x, c, ctx, c_ctx, ada_w, ada_b, pre_g, post_g, ab_w_in, ab_sink, ab_ln_g, ab_ln_b, ab_ws, ab_ws_b,
           ab_w_out, cv_w_in, cv_dw, cv_dw_b, cv_ln_g, cv_ln_b, cv_w_out):
    assert x.shape == (1, SEQ, D_MODEL) and ctx.shape == (1, CTX_LEN, D_MODEL)
    cc = jnp.concatenate([c, c_ctx[None, :], jnp.zeros((6, D_MODEL), F32)], axis=0)
    mod = _modulation(cc, ada_w, ada_b)
    rope_c, rope_s1, rope_s2 = _rope_tables()
    pre_g = pre_g.reshape(DEPTH, 1, D_MODEL)
    post_g = post_g.reshape(DEPTH, 1, D_MODEL)
    vec3 = lambda t: t.reshape(t.shape[0], 1, t.shape[1])

    h = _first_prenorm(x, ctx, mod, pre_g)
    xs = (x, ctx)
    y = None
    for layer in range(DEPTH):
        last = layer == DEPTH - 1
        i = layer // 2
        rows = SEQ if last else ROWS
        tm = TM_LATENT if last else TM_ALL
        if layer > 0:
            x_all, h = _post_pre(xs, y, mod, post_g, pre_g, layer, rows)
            xs = (x_all,)
        if layer % 2 == 0:
            proj = _inproj_even(h, ab_w_in, i, rope_c, rope_s1, rope_s2, tm)
            sink_b = jnp.broadcast_to(ab_sink[i][:, None], (N_Q_HEADS, HEAD_DIM))
            attn = _attention(proj, sink_b)
            wsb_b = jnp.broadcast_to(ab_ws_b[i][:, :, None], (GMLP_GROUPS, BLOCK, BLOCK))
            mix = _gmlp(proj, ab_ws, wsb_b, vec3(ab_ln_g), vec3(ab_ln_b), i)
            y = _outproj(attn, mix, 0, ab_w_out, i, rows, tm)
        else:
            glu, sg = _inproj_conv(h, cv_w_in, i, tm)
            z = _conv_mix(glu, sg, cv_dw, vec3(cv_dw_b), vec3(cv_ln_g), vec3(cv_ln_b), i, rows)
            y = _outproj(z, z, 1, cv_w_out, i, rows, tm)
    out = _final_post(x_all, y, mod, post_g, DEPTH - 1, SEQ)
    return out[None]
```

```python
import functools

import jax
import jax.numpy as jnp
import numpy as np
from jax import lax
from jax.experimental import pallas as pl
from jax.experimental.pallas import tpu as pltpu

D_MODEL = 4096
SEQ = 8192
DEPTH = 4
GRID_W = 64
CTX_LEN = 256
ROWS = SEQ + CTX_LEN
HEAD_DIM = 128
N_Q_HEADS = 16
N_KV_HEADS = 4
GQA_GROUP = 4
ATTN_W = 2048
KV_W = 512
BLOCK = 128
N_BLOCKS = SEQ // BLOCK
GMLP_GROUPS = 16
GMLP_W = 2048
AB_IN_W = 11264
CONV_K = 31
CONV_HALO = 16
ROPE_BASE = 10000.0
RMS_EPS = 1e-6
LN_EPS = 1e-5
NEG_INF = -1e30

F32 = jnp.float32
BF16 = jnp.bfloat16
MIB = 1024 * 1024
LANES = 128
SUBLANES = 8
MXU_N = 256

ROW_TILE = 256
CAST_ROWS = 256
TM_ALL = 1056
TM_LATENT = 1024
TM_WIDE_ALL = 528
TM_WIDE_LATENT = 512
NORM_ROWS = 16

IN_TN = 1024
J_Q = (0, 2)
J_GA = (2, 4)
J_U = (4, 6)
J_VG = (6, 8)
J_GB = (8, 10)
J_KV = 10


def _params(semantics, vmem_mib):
    return pltpu.CompilerParams(dimension_semantics=semantics, vmem_limit_bytes=vmem_mib * MIB)


def _silu(t):
    return t * jax.nn.sigmoid(t)


def _gelu(t):
    return 0.5 * t * (1.0 + lax.erf(t * np.float32(np.sqrt(0.5))))


def _rms(t):
    return t * lax.rsqrt(jnp.mean(t * t, axis=-1, keepdims=True) + RMS_EPS)


def _cast_weight_once(w_ref, wb_ref):
    @pl.when(pl.program_id(1) == 0)
    def _():
        def rows(r, carry):
            rs = pl.ds(pl.multiple_of(r * CAST_ROWS, CAST_ROWS), CAST_ROWS)
            wb_ref[rs, :] = w_ref[rs, :].astype(BF16)
            return carry
        lax.fori_loop(0, w_ref.shape[0] // CAST_ROWS, rows, 0)


def _mod_kernel(cc_ref, w_ref, b_ref, o_ref):
    a = _silu(cc_ref[...]).astype(BF16)
    w = w_ref[...].astype(BF16)
    o_ref[...] = jnp.dot(a, w, preferred_element_type=F32) + b_ref[...]


def _modulation(cc, ada_w, ada_b):
    tn = 512
    n3 = 3 * D_MODEL
    return pl.pallas_call(
        _mod_kernel,
        grid=(DEPTH, n3 // tn),
        in_specs=[
            pl.BlockSpec((8, D_MODEL), lambda l, j: (0, 0)),
            pl.BlockSpec((None, D_MODEL, tn), lambda l, j: (l, 0, j)),
            pl.BlockSpec((None, 1, tn), lambda l, j: (l, 0, j)),
        ],
        out_specs=pl.BlockSpec((None, 8, tn), lambda l, j: (l, 0, j)),
        out_shape=jax.ShapeDtypeStruct((DEPTH, 8, n3), F32),
        compiler_params=_params(("arbitrary", "arbitrary"), 40),
        name="adaln_mod",
    )(cc, ada_w, ada_b.reshape(DEPTH, 1, n3))


def _modulate(x, mod_ref, g_ref, is_ctx):
    shift = jnp.where(is_ctx, mod_ref[1:2, 0:D_MODEL], mod_ref[0:1, 0:D_MODEL])
    scale = jnp.where(is_ctx, mod_ref[1:2, D_MODEL:2 * D_MODEL], mod_ref[0:1, D_MODEL:2 * D_MODEL])
    return (_rms(x) * g_ref[...] * (1.0 + scale) + shift).astype(BF16)


def _residual(x, y_ref, mod_ref, pg_ref, is_ctx):
    gate = jnp.where(is_ctx, mod_ref[1:2, 2 * D_MODEL:3 * D_MODEL], mod_ref[0:1, 2 * D_MODEL:3 * D_MODEL])
    return x + gate * (_rms(y_ref[...].astype(F32)) * pg_ref[...])


def _first_prenorm_kernel(x_ref, ctx_ref, mod_ref, g_ref, h_ref):
    is_ctx = pl.program_id(0) * ROW_TILE >= SEQ
    x = jnp.where(is_ctx, ctx_ref[...], x_ref[...])
    h_ref[...] = _modulate(x, mod_ref, g_ref, is_ctx)


def _post_pre_kernel(x_ref, y_ref, modp_ref, pg_ref, mod_ref, g_ref, xo_ref, h_ref):
    is_ctx = pl.program_id(0) * ROW_TILE >= SEQ
    x = _residual(x_ref[...], y_ref, modp_ref, pg_ref, is_ctx)
    xo_ref[...] = x
    h_ref[...] = _modulate(x, mod_ref, g_ref, is_ctx)


def _post_pre_split_kernel(x_ref, ctx_ref, y_ref, modp_ref, pg_ref, mod_ref, g_ref, xo_ref, h_ref):
    is_ctx = pl.program_id(0) * ROW_TILE >= SEQ
    x = _residual(jnp.where(is_ctx, ctx_ref[...], x_ref[...]), y_ref, modp_ref, pg_ref, is_ctx)
    xo_ref[...] = x
    h_ref[...] = _modulate(x, mod_ref, g_ref, is_ctx)


def _final_post_kernel(x_ref, y_ref, modp_ref, pg_ref, xo_ref):
    xo_ref[...] = _residual(x_ref[...], y_ref, modp_ref, pg_ref, False)


def _row_spec():
    return pl.BlockSpec((ROW_TILE, D_MODEL), lambda i: (i, 0))


def _mod_spec(layer):
    return pl.BlockSpec((None, 8, 3 * D_MODEL), lambda i: (layer, 0, 0))


def _gain_spec(layer):
    return pl.BlockSpec((None, 1, D_MODEL), lambda i: (layer, 0, 0))


def _split_input_specs():
    last_x = SEQ // ROW_TILE - 1
    return [pl.BlockSpec((None, ROW_TILE, D_MODEL), lambda i: (0, jnp.minimum(i, last_x), 0)),
            pl.BlockSpec((None, ROW_TILE, D_MODEL), lambda i: (0, jnp.maximum(i - last_x - 1, 0), 0))]


def _first_prenorm(x, ctx, mod, pre_g):
    return pl.pallas_call(
        _first_prenorm_kernel,
        grid=(ROWS // ROW_TILE,),
        in_specs=_split_input_specs() + [_mod_spec(0), _gain_spec(0)],
        out_specs=_row_spec(),
        out_shape=jax.ShapeDtypeStruct((ROWS, D_MODEL), BF16),
        compiler_params=_params(("arbitrary",), 48),
        name="first_prenorm",
    )(x, ctx, mod, pre_g)


def _post_pre(xs, y, mod, post_g, pre_g, layer, rows):
    split = len(xs) == 2
    return pl.pallas_call(
        _post_pre_split_kernel if split else _post_pre_kernel,
        grid=(rows // ROW_TILE,),
        in_specs=(_split_input_specs() if split else [_row_spec()])
        + [_row_spec(), _mod_spec(layer - 1), _gain_spec(layer - 1), _mod_spec(layer), _gain_spec(layer)],
        out_specs=[_row_spec(), _row_spec()],
        out_shape=[jax.ShapeDtypeStruct((rows, D_MODEL), F32), jax.ShapeDtypeStruct((rows, D_MODEL), BF16)],
        compiler_params=_params(("arbitrary",), 48),
        name="post_pre_norm",
    )(*xs, y, mod, post_g, mod, pre_g)


def _final_post(x_all, y, mod, post_g, layer, rows):
    return pl.pallas_call(
        _final_post_kernel,
        grid=(rows // ROW_TILE,),
        in_specs=[_row_spec(), _row_spec(), _mod_spec(layer), _gain_spec(layer)],
        out_specs=_row_spec(),
        out_shape=jax.ShapeDtypeStruct((rows, D_MODEL), F32),
        compiler_params=_params(("arbitrary",), 48),
        name="final_post_norm",
    )(x_all, y, mod, post_g)


def _inproj_even_kernel(h_ref, w_ref, c_ref, s1_ref, s2_ref, o_ref, wb_ref):
    j = pl.program_id(0)
    _cast_weight_once(w_ref, wb_ref)

    def column_parts(*epilogues):
        parts = IN_TN // MXU_N
        for p in range(parts):
            cs = slice(p * MXU_N, (p + 1) * MXU_N)
            acc = jnp.dot(h_ref[...], wb_ref[:, cs], preferred_element_type=F32)
            epilogues[p * len(epilogues) // parts](acc, p * MXU_N)

    def rope(acc, c0):
        c = c_ref[...]
        s1 = s1_ref[...]
        s2 = s2_ref[...]
        for hh in range(MXU_N // HEAD_DIM):
            xs = acc[:, hh * HEAD_DIM:(hh + 1) * HEAD_DIM]
            up = pltpu.roll(xs, HEAD_DIM - 32, axis=1)
            dn = pltpu.roll(xs, 32, axis=1)
            o_ref[:, c0 + hh * HEAD_DIM:c0 + (hh + 1) * HEAD_DIM] = (xs * c + up * s1 + dn * s2).astype(BF16)

    def store(fn):
        def epilogue(acc, c0):
            o_ref[:, c0:c0 + MXU_N] = fn(acc).astype(BF16)
        return epilogue

    @pl.when(j < J_Q[1])
    def _():
        column_parts(rope)

    @pl.when(j == J_KV)
    def _():
        column_parts(rope, store(lambda t: t))

    @pl.when(((j >= J_GA[0]) & (j < J_GA[1])) | ((j >= J_GB[0]) & (j < J_GB[1])))
    def _():
        column_parts(store(_silu))

    @pl.when((j >= J_U[0]) & (j < J_VG[1]))
    def _():
        column_parts(store(_gelu))


def _even_weight_block(j):
    return jnp.where(j < J_Q[1], j, jnp.where(j < J_KV, j + 1, J_Q[1]))


def _inproj_even(h, w_in, layer_i, rope_c, rope_s1, rope_s2, tm):
    rows = h.shape[0]
    tbl = pl.BlockSpec((tm, HEAD_DIM), lambda j, i: (i, 0))
    return pl.pallas_call(
        _inproj_even_kernel,
        grid=(AB_IN_W // IN_TN, rows // tm),
        in_specs=[
            pl.BlockSpec((tm, D_MODEL), lambda j, i: (i, 0)),
            pl.BlockSpec((None, D_MODEL, IN_TN), lambda j, i: (layer_i, 0, _even_weight_block(j))),
            tbl, tbl, tbl,
        ],
        out_specs=pl.BlockSpec((tm, IN_TN), lambda j, i: (i, j)),
        out_shape=jax.ShapeDtypeStruct((rows, AB_IN_W), BF16),
        scratch_shapes=[pltpu.VMEM((D_MODEL, IN_TN), BF16)],
        compiler_params=_params(("arbitrary", "arbitrary"), 58),
        name="inproj_even",
    )(h, w_in, rope_c, rope_s1, rope_s2)


def _dot_nt(a, b):
    return lax.dot_general(a, b, (((1,), (1,)), ((), ())), preferred_element_type=F32)


def _attn_kernel(q_ref, ga_ref, kp_ref, km_ref, kn_ref, vp_ref, vm_ref, vn_ref, kc_ref, vc_ref,
                 sink_ref, o_ref):
    n = pl.program_id(0)
    scale = np.float32(HEAD_DIM ** -0.5)
    grp = GQA_GROUP * BLOCK

    def group_queries(h):
        heads = [h * GQA_GROUP + g for g in range(GQA_GROUP)]
        q = jnp.concatenate([q_ref[:, hq * HEAD_DIM:(hq + 1) * HEAD_DIM] for hq in heads], axis=0)
        snk = jnp.concatenate([jnp.broadcast_to(sink_ref[hq:hq + 1, 0:1], (BLOCK, 1)) for hq in heads], axis=0)
        return q, snk

    def softmax_av(h, s, snk, v):
        m = jnp.maximum(jnp.max(s, axis=-1, keepdims=True), snk)
        p = jnp.exp(s - m)
        l = jnp.sum(p, axis=-1, keepdims=True) + jnp.exp(snk - m)
        o = jnp.dot(p.astype(BF16), v, preferred_element_type=F32) / l
        for g in range(GQA_GROUP):
            sl = slice((h * GQA_GROUP + g) * HEAD_DIM, (h * GQA_GROUP + g + 1) * HEAD_DIM)
            o_ref[:, sl] = (o[g * BLOCK:(g + 1) * BLOCK] * ga_ref[:, sl].astype(F32)).astype(BF16)

    @pl.when(n < N_BLOCKS)
    def _():
        ri = lax.broadcasted_iota(jnp.int32, (grp, BLOCK), 0) & (BLOCK - 1)
        cj = lax.broadcasted_iota(jnp.int32, (grp, BLOCK), 1)
        mask_p = (cj >= ri) & (n > 0)
        mask_n = (cj <= ri) & (n < N_BLOCKS - 1)
        for h in range(N_KV_HEADS):
            ks = slice(h * HEAD_DIM, (h + 1) * HEAD_DIM)
            k = jnp.concatenate([kp_ref[:, ks], km_ref[:, ks], kn_ref[:, ks], kc_ref[:, ks]], axis=0)
            v = jnp.concatenate([vp_ref[:, ks], vm_ref[:, ks], vn_ref[:, ks], vc_ref[:, ks]], axis=0)
            q, snk = group_queries(h)
            s = _dot_nt(q, k) * scale
            s = jnp.concatenate([jnp.where(mask_p, s[:, 0:BLOCK], NEG_INF), s[:, BLOCK:2 * BLOCK],
                                 jnp.where(mask_n, s[:, 2 * BLOCK:3 * BLOCK], NEG_INF), s[:, 3 * BLOCK:]], axis=1)
            softmax_av(h, s, snk, v)

    @pl.when(n >= N_BLOCKS)
    def _():
        for h in range(N_KV_HEADS):
            ks = slice(h * HEAD_DIM, (h + 1) * HEAD_DIM)
            q, snk = group_queries(h)
            softmax_av(h, _dot_nt(q, kc_ref[:, ks]) * scale, snk, vc_ref[:, ks])


def _attention(proj, sink_b):
    rows = proj.shape[0]
    kcol = J_KV * IN_TN // KV_W
    vcol = kcol + 1
    ctx_blk = SEQ // CTX_LEN
    lo = lambda n: jnp.clip(n - 1, 0, N_BLOCKS - 1)
    mid = lambda n: jnp.minimum(n, N_BLOCKS - 1)
    hi = lambda n: jnp.clip(n + 1, 0, N_BLOCKS - 1)
    kv = lambda f, col: pl.BlockSpec((BLOCK, KV_W), lambda n: (f(n), col))
    return pl.pallas_call(
        _attn_kernel,
        grid=(rows // BLOCK,),
        in_specs=[
            pl.BlockSpec((BLOCK, ATTN_W), lambda n: (n, J_Q[0] * IN_TN // ATTN_W)),
            pl.BlockSpec((BLOCK, ATTN_W), lambda n: (n, J_GA[0] * IN_TN // ATTN_W)),
            kv(lo, kcol), kv(mid, kcol), kv(hi, kcol),
            kv(lo, vcol), kv(mid, vcol), kv(hi, vcol),
            pl.BlockSpec((CTX_LEN, KV_W), lambda n: (ctx_blk, kcol)),
            pl.BlockSpec((CTX_LEN, KV_W), lambda n: (ctx_blk, vcol)),
            pl.BlockSpec((N_Q_HEADS, HEAD_DIM), lambda n: (0, 0)),
        ],
        out_specs=pl.BlockSpec((BLOCK, ATTN_W), lambda n: (n, 0)),
        out_shape=jax.ShapeDtypeStruct((rows, ATTN_W), BF16),
        compiler_params=_params(("arbitrary",), 40),
        name="band_attention",
    )(proj, proj, proj, proj, proj, proj, proj, proj, proj, proj, sink_b)


def _gmlp_kernel(u_ref, vg_ref, gb_ref, ws_ref, wsb_ref, lg_ref, lb_ref, o_ref):
    for ch in range(ROW_TILE // BLOCK):
        rs = slice(ch * BLOCK, (ch + 1) * BLOCK)
        v = vg_ref[rs, :].astype(F32)
        mu = jnp.mean(v, axis=-1, keepdims=True)
        vc = v - mu
        var = jnp.mean(vc * vc, axis=-1, keepdims=True)
        y = (vc * lax.rsqrt(var + LN_EPS) * lg_ref[...] + lb_ref[...]).astype(BF16)
        for g in range(GMLP_GROUPS):
            sl = slice(g * HEAD_DIM, (g + 1) * HEAD_DIM)
            s = jnp.dot(ws_ref[g].astype(BF16), y[:, sl], preferred_element_type=F32) + wsb_ref[g]
            o_ref[rs, sl] = (u_ref[rs, sl].astype(F32) * s * gb_ref[rs, sl].astype(F32)).astype(BF16)


def _gmlp(proj, ws, wsb_b, ln_g, ln_b, layer_i):
    rows = proj.shape[0]
    col = lambda c: pl.BlockSpec((ROW_TILE, GMLP_W), lambda n: (n, c))
    vec = pl.BlockSpec((None, 1, GMLP_W), lambda n: (layer_i, 0, 0))
    return pl.pallas_call(
        _gmlp_kernel,
        grid=(rows // ROW_TILE,),
        in_specs=[
            col(J_U[0] * IN_TN // GMLP_W), col(J_VG[0] * IN_TN // GMLP_W), col(J_GB[0] * IN_TN // GMLP_W),
            pl.BlockSpec((None, GMLP_GROUPS, BLOCK, BLOCK), lambda n: (layer_i, 0, 0, 0)),
            pl.BlockSpec((GMLP_GROUPS, BLOCK, BLOCK), lambda n: (0, 0, 0)),
            vec, vec,
        ],
        out_specs=pl.BlockSpec((ROW_TILE, GMLP_W), lambda n: (n, 0)),
        out_shape=jax.ShapeDtypeStruct((rows, GMLP_W), BF16),
        compiler_params=_params(("arbitrary",), 40),
        name="chunk_gmlp",
    )(proj, proj, proj, ws, wsb_b, ln_g, ln_b)


def _outproj_kernel(a1_ref, a2_ref, w_ref, y_ref, wb_ref):
    _cast_weight_once(w_ref, wb_ref)
    half = D_MODEL // 2
    for hf in range(y_ref.shape[1] // MXU_N):
        cs = slice(hf * MXU_N, (hf + 1) * MXU_N)
        acc = (jnp.dot(a1_ref[...], wb_ref[0:half, cs], preferred_element_type=F32)
               + jnp.dot(a2_ref[...], wb_ref[half:D_MODEL, cs], preferred_element_type=F32))
        y_ref[:, cs] = acc.astype(BF16)


def _outproj(a1, a2, a2_col, w, layer_i, rows, tm):
    tn = IN_TN
    half = D_MODEL // 2
    return pl.pallas_call(
        _outproj_kernel,
        grid=(D_MODEL // tn, rows // tm),
        in_specs=[
            pl.BlockSpec((tm, half), lambda j, i: (i, 0)),
            pl.BlockSpec((tm, half), lambda j, i: (i, a2_col)),
            pl.BlockSpec((None, D_MODEL, tn), lambda j, i: (layer_i, 0, j)),
        ],
        out_specs=pl.BlockSpec((tm, tn), lambda j, i: (i, j)),
        out_shape=jax.ShapeDtypeStruct((rows, D_MODEL), BF16),
        scratch_shapes=[pltpu.VMEM((D_MODEL, tn), BF16)],
        compiler_params=_params(("arbitrary", "arbitrary"), 58),
        name="outproj",
    )(a1, a2, w)


def _inproj_conv_kernel(h_ref, wa_ref, wb_ref, wg_ref, glu_ref, sg_ref, wab_ref, wbb_ref, wgb_ref):
    _cast_weight_once(wa_ref, wab_ref)
    _cast_weight_once(wb_ref, wbb_ref)
    _cast_weight_once(wg_ref, wgb_ref)
    h = h_ref[...]
    a = jnp.dot(h, wab_ref[...], preferred_element_type=F32)
    b = jnp.dot(h, wbb_ref[...], preferred_element_type=F32)
    glu_ref[...] = (a * jax.nn.sigmoid(b)).astype(BF16)
    g = jnp.dot(h, wgb_ref[...], preferred_element_type=F32)
    sg_ref[...] = _silu(g).astype(BF16)


def _inproj_conv(h, w, layer_i, tm):
    rows = h.shape[0]
    tn = MXU_N
    nj = D_MODEL // tn
    wspec = lambda part: pl.BlockSpec((None, D_MODEL, tn), lambda j, i: (layer_i, 0, j + part * nj))
    out = pl.BlockSpec((tm, tn), lambda j, i: (i, j))
    return pl.pallas_call(
        _inproj_conv_kernel,
        grid=(nj, rows // tm),
        in_specs=[pl.BlockSpec((tm, D_MODEL), lambda j, i: (i, 0)), wspec(0), wspec(1), wspec(2)],
        out_specs=[out, out],
        out_shape=[jax.ShapeDtypeStruct((rows, D_MODEL), BF16)] * 2,
        scratch_shapes=[pltpu.VMEM((D_MODEL, tn), BF16)] * 3,
        compiler_params=_params(("arbitrary", "arbitrary"), 60),
        name="inproj_conv",
    )(h, w, w, w)


def _conv_kernel(gp_ref, gm_ref, gn_ref, sg_ref, dw_ref, dwb_ref, lg_ref, lb_ref, o_ref, y_ref, s1_ref, sh_ref):
    i = pl.program_id(0)
    tr = ROW_TILE
    first_ctx = SEQ // tr
    has_prev = (i != 0) & (i != first_ctx)
    has_next = (i != first_ctx - 1) & (i != pl.num_programs(0) - 1)
    base = CONV_HALO - CONV_K // 2
    sub = 128
    s1_ref[...] = jnp.zeros_like(s1_ref)

    def lane_block(c, carry):
        ls = pl.ds(pl.multiple_of(c * LANES, LANES), LANES)
        prev = jnp.where(has_prev, gp_ref[:, ls].astype(F32), 0.0)
        nxt = jnp.where(has_next, gn_ref[:, ls].astype(F32), 0.0)
        window = jnp.concatenate([prev, gm_ref[:, ls].astype(F32), nxt], axis=0)
        sh_ref[0] = window
        for s in range(1, SUBLANES):
            sh_ref[s] = pltpu.roll(window, window.shape[0] - s, axis=0)
        for r0 in range(0, tr, sub):
            acc = jnp.broadcast_to(dwb_ref[:, ls], (sub, LANES))
            for d in range(CONV_K):
                q, s = divmod(base + d, SUBLANES)
                acc = acc + sh_ref[s, r0 + q * SUBLANES:r0 + q * SUBLANES + sub, :] * dw_ref[d:d + 1, ls]
            y_ref[r0:r0 + sub, ls] = acc
            s1_ref[r0:r0 + sub, :] += acc
        return carry

    lax.fori_loop(0, D_MODEL // LANES, lane_block, 0)

    def norm_rows(r, carry):
        rs = pl.ds(pl.multiple_of(r * NORM_ROWS, NORM_ROWS), NORM_ROWS)
        mu = jnp.sum(s1_ref[rs, :], axis=-1, keepdims=True) * np.float32(1.0 / D_MODEL)
        yc = y_ref[rs, :] - mu
        var = jnp.mean(yc * yc, axis=-1, keepdims=True)
        z = yc * lax.rsqrt(var + LN_EPS) * lg_ref[...] + lb_ref[...]
        o_ref[rs, :] = (_silu(z) * sg_ref[rs, :].astype(F32)).astype(BF16)
        return carry

    lax.fori_loop(0, tr // NORM_ROWS, norm_rows, 0, unroll=4)


def _conv_mix(glu, sg, dw, dw_b, ln_g, ln_b, layer_i, rows):
    tr = ROW_TILE
    per = tr // CONV_HALO
    n_halo = glu.shape[0] // CONV_HALO
    vec = pl.BlockSpec((None, 1, D_MODEL), lambda i: (layer_i, 0, 0))
    return pl.pallas_call(
        _conv_kernel,
        grid=(rows // tr,),
        in_specs=[
            pl.BlockSpec((CONV_HALO, D_MODEL), lambda i: (jnp.maximum(i * per - 1, 0), 0)),
            pl.BlockSpec((tr, D_MODEL), lambda i: (i, 0)),
            pl.BlockSpec((CONV_HALO, D_MODEL), lambda i: (jnp.minimum((i + 1) * per, n_halo - 1), 0)),
            pl.BlockSpec((tr, D_MODEL), lambda i: (i, 0)),
            pl.BlockSpec((None, CONV_K, D_MODEL), lambda i: (layer_i, 0, 0)),
            vec, vec, vec,
        ],
        out_specs=pl.BlockSpec((tr, D_MODEL), lambda i: (i, 0)),
        out_shape=jax.ShapeDtypeStruct((rows, D_MODEL), BF16),
        scratch_shapes=[pltpu.VMEM((tr, D_MODEL), F32), pltpu.VMEM((tr, LANES), F32),
                        pltpu.VMEM((SUBLANES, tr + 2 * CONV_HALO, LANES), F32)],
        compiler_params=_params(("arbitrary",), 48),
        name="conv_ln_swish",
    )(glu, glu, glu, sg, dw, dw_b, ln_g, ln_b)


def _rope_tables():
    quarter = HEAD_DIM // 4
    inv = (1.0 / (ROPE_BASE ** (np.arange(quarter, dtype=np.float32) / quarter))).astype(np.float32)
    pos = np.arange(SEQ, dtype=np.int32)
    row = (pos // GRID_W).astype(np.float32)[:, None] * inv[None, :]
    col = (pos % GRID_W).astype(np.float32)[:, None] * inv[None, :]
    zero = np.zeros_like(row)
    cos = np.concatenate([np.cos(row), np.cos(row), np.cos(col), np.cos(col)], axis=-1)
    s_up = np.concatenate([-np.sin(row), zero, -np.sin(col), zero], axis=-1)
    s_dn = np.concatenate([zero, np.sin(row), zero, np.sin(col)], axis=-1)
    pad = lambda t, v: np.concatenate([t, np.full((CTX_LEN, HEAD_DIM), v, np.float32)], axis=0).astype(np.float32)
    return pad(cos, 1.0), pad(s_up, 0.0), pad(s_dn, 0.0)


def kernel(x, c, ctx, c_ctx, ada_w, ada_b, pre_g, post_g, ab_w_in, ab_sink, ab_ln_g, ab_ln_b, ab_ws, ab_ws_b,
           ab_w_out, cv_w_in, cv_dw, cv_dw_b, cv_ln_g, cv_ln_b, cv_w_out):
    assert x.shape == (1, SEQ, D_MODEL) and ctx.shape == (1, CTX_LEN, D_MODEL)
    cc = jnp.concatenate([c, c_ctx[None, :], jnp.zeros((6, D_MODEL), F32)], axis=0)
    mod = _modulation(cc, ada_w, ada_b)
    rope_c, rope_s1, rope_s2 = _rope_tables()
    pre_g = pre_g.reshape(DEPTH, 1, D_MODEL)
    post_g = post_g.reshape(DEPTH, 1, D_MODEL)
    vec3 = lambda t: t.reshape(t.shape[0], 1, t.shape[1])

    h = _first_prenorm(x, ctx, mod, pre_g)
    xs = (x, ctx)
    y = None
    for layer in range(DEPTH):
        last = layer == DEPTH - 1
        i = layer // 2
        rows = SEQ if last else ROWS
        tm = TM_LATENT if last else TM_ALL
        tm_wide = TM_WIDE_LATENT if last else TM_WIDE_ALL
        if layer > 0:
            x_all, h = _post_pre(xs, y, mod, post_g, pre_g, layer, rows)
            xs = (x_all,)
        if layer % 2 == 0:
            proj = _inproj_even(h, ab_w_in, i, rope_c, rope_s1, rope_s2, tm_wide)
            sink_b = jnp.broadcast_to(ab_sink[i][:, None], (N_Q_HEADS, HEAD_DIM))
            attn = _attention(proj, sink_b)
            wsb_b = jnp.broadcast_to(ab_ws_b[i][:, :, None], (GMLP_GROUPS, BLOCK, BLOCK))
            mix = _gmlp(proj, ab_ws, wsb_b, vec3(ab_ln_g), vec3(ab_ln_b), i)
            y = _outproj(attn, mix, 0, ab_w_out, i, rows, tm_wide)
        else:
            glu, sg = _inproj_conv(h, cv_w_in, i, tm)
            z = _conv_mix(glu, sg, cv_dw, vec3(cv_dw_b), vec3(cv_ln_g), vec3(cv_ln_b), i, rows)
            y = _outproj(z, z, 1, cv_w_out, i, rows, tm_wide)
    out = _final_post(x_all, y, mod, post_g, DEPTH - 1, SEQ)
    return out[None]
```

```python
import functools

import jax
import jax.numpy as jnp
import numpy as np
from jax import lax
from jax.experimental import pallas as pl
from jax.experimental.pallas import tpu as pltpu

D_MODEL = 4096
SEQ = 8192
DEPTH = 4
GRID_W = 64
CTX_LEN = 256
ROWS = SEQ + CTX_LEN
HEAD_DIM = 128
N_Q_HEADS = 16
N_KV_HEADS = 4
GQA_GROUP = 4
ATTN_W = 2048
KV_W = 512
BLOCK = 128
N_BLOCKS = SEQ // BLOCK
GMLP_GROUPS = 16
GMLP_W = 2048
AB_IN_W = 11264
CONV_K = 31
CONV_HALO = 16
ROPE_BASE = 10000.0
RMS_EPS = 1e-6
LN_EPS = 1e-5
NEG_INF = -1e30

F32 = jnp.float32
BF16 = jnp.bfloat16
MIB = 1024 * 1024
LANES = 128
SUBLANES = 8
MXU_N = 256

ROW_TILE = 256
CAST_ROWS = 256
TM_ALL = 1056
TM_LATENT = 1024
TM_WIDE_ALL = 528
TM_WIDE_LATENT = 512
NORM_ROWS = 16

IN_TN = 1024
J_Q = (0, 2)
J_GA = (2, 4)
J_U = (4, 6)
J_VG = (6, 8)
J_GB = (8, 10)
J_KV = 10


def _params(semantics, vmem_mib):
    return pltpu.CompilerParams(dimension_semantics=semantics, vmem_limit_bytes=vmem_mib * MIB)


def _silu(t):
    return t * jax.nn.sigmoid(t)


def _gelu(t):
    return 0.5 * t * (1.0 + lax.erf(t * np.float32(np.sqrt(0.5))))


def _rms(t):
    return t * lax.rsqrt(jnp.mean(t * t, axis=-1, keepdims=True) + RMS_EPS)


def _cast_weight_once(w_ref, wb_ref):
    @pl.when(pl.program_id(1) == 0)
    def _():
        def rows(r, carry):
            rs = pl.ds(pl.multiple_of(r * CAST_ROWS, CAST_ROWS), CAST_ROWS)
            wb_ref[rs, :] = w_ref[rs, :].astype(BF16)
            return carry
        lax.fori_loop(0, w_ref.shape[0] // CAST_ROWS, rows, 0)


def _mod_kernel(cc_ref, w_ref, b_ref, o_ref):
    a = _silu(cc_ref[...]).astype(BF16)
    w = w_ref[...].astype(BF16)
    o_ref[...] = jnp.dot(a, w, preferred_element_type=F32) + b_ref[...]


def _modulation(cc, ada_w, ada_b):
    tn = 512
    n3 = 3 * D_MODEL
    return pl.pallas_call(
        _mod_kernel,
        grid=(DEPTH, n3 // tn),
        in_specs=[
            pl.BlockSpec((8, D_MODEL), lambda l, j: (0, 0)),
            pl.BlockSpec((None, D_MODEL, tn), lambda l, j: (l, 0, j)),
            pl.BlockSpec((None, 1, tn), lambda l, j: (l, 0, j)),
        ],
        out_specs=pl.BlockSpec((None, 8, tn), lambda l, j: (l, 0, j)),
        out_shape=jax.ShapeDtypeStruct((DEPTH, 8, n3), F32),
        compiler_params=_params(("arbitrary", "arbitrary"), 40),
        name="adaln_mod",
    )(cc, ada_w, ada_b.reshape(DEPTH, 1, n3))


def _modulate(x, mod_ref, g_ref, is_ctx):
    shift = jnp.where(is_ctx, mod_ref[1:2, 0:D_MODEL], mod_ref[0:1, 0:D_MODEL])
    scale = jnp.where(is_ctx, mod_ref[1:2, D_MODEL:2 * D_MODEL], mod_ref[0:1, D_MODEL:2 * D_MODEL])
    return (_rms(x) * g_ref[...] * (1.0 + scale) + shift).astype(BF16)


def _residual(x, y_ref, mod_ref, pg_ref, is_ctx):
    gate = jnp.where(is_ctx, mod_ref[1:2, 2 * D_MODEL:3 * D_MODEL], mod_ref[0:1, 2 * D_MODEL:3 * D_MODEL])
    return x + gate * (_rms(y_ref[...].astype(F32)) * pg_ref[...])


def _first_prenorm_kernel(x_ref, ctx_ref, mod_ref, g_ref, h_ref):
    is_ctx = pl.program_id(0) * ROW_TILE >= SEQ
    x = jnp.where(is_ctx, ctx_ref[...], x_ref[...])
    h_ref[...] = _modulate(x, mod_ref, g_ref, is_ctx)


def _post_pre_kernel(x_ref, y_ref, modp_ref, pg_ref, mod_ref, g_ref, xo_ref, h_ref):
    is_ctx = pl.program_id(0) * ROW_TILE >= SEQ
    x = _residual(x_ref[...], y_ref, modp_ref, pg_ref, is_ctx)
    xo_ref[...] = x
    h_ref[...] = _modulate(x, mod_ref, g_ref, is_ctx)


def _post_pre_split_kernel(x_ref, ctx_ref, y_ref, modp_ref, pg_ref, mod_ref, g_ref, xo_ref, h_ref):
    is_ctx = pl.program_id(0) * ROW_TILE >= SEQ
    x = _residual(jnp.where(is_ctx, ctx_ref[...], x_ref[...]), y_ref, modp_ref, pg_ref, is_ctx)
    xo_ref[...] = x
    h_ref[...] = _modulate(x, mod_ref, g_ref, is_ctx)


def _final_post_kernel(x_ref, y_ref, modp_ref, pg_ref, xo_ref):
    xo_ref[...] = _residual(x_ref[...], y_ref, modp_ref, pg_ref, False)


def _row_spec():
    return pl.BlockSpec((ROW_TILE, D_MODEL), lambda i: (i, 0))


def _mod_spec(layer):
    return pl.BlockSpec((None, 8, 3 * D_MODEL), lambda i: (layer, 0, 0))


def _gain_spec(layer):
    return pl.BlockSpec((None, 1, D_MODEL), lambda i: (layer, 0, 0))


def _split_input_specs():
    last_x = SEQ // ROW_TILE - 1
    return [pl.BlockSpec((None, ROW_TILE, D_MODEL), lambda i: (0, jnp.minimum(i, last_x), 0)),
            pl.BlockSpec((None, ROW_TILE, D_MODEL), lambda i: (0, jnp.maximum(i - last_x - 1, 0), 0))]


def _first_prenorm(x, ctx, mod, pre_g):
    return pl.pallas_call(
        _first_prenorm_kernel,
        grid=(ROWS // ROW_TILE,),
        in_specs=_split_input_specs() + [_mod_spec(0), _gain_spec(0)],
        out_specs=_row_spec(),
        out_shape=jax.ShapeDtypeStruct((ROWS, D_MODEL), BF16),
        compiler_params=_params(("arbitrary",), 48),
        name="first_prenorm",
    )(x, ctx, mod, pre_g)


def _post_pre(xs, y, mod, post_g, pre_g, layer, rows):
    split = len(xs) == 2
    return pl.pallas_call(
        _post_pre_split_kernel if split else _post_pre_kernel,
        grid=(rows // ROW_TILE,),
        in_specs=(_split_input_specs() if split else [_row_spec()])
        + [_row_spec(), _mod_spec(layer - 1), _gain_spec(layer - 1), _mod_spec(layer), _gain_spec(layer)],
        out_specs=[_row_spec(), _row_spec()],
        out_shape=[jax.ShapeDtypeStruct((rows, D_MODEL), F32), jax.ShapeDtypeStruct((rows, D_MODEL), BF16)],
        compiler_params=_params(("arbitrary",), 48),
        name="post_pre_norm",
    )(*xs, y, mod, post_g, mod, pre_g)


def _final_post(x_all, y, mod, post_g, layer, rows):
    return pl.pallas_call(
        _final_post_kernel,
        grid=(rows // ROW_TILE,),
        in_specs=[_row_spec(), _row_spec(), _mod_spec(layer), _gain_spec(layer)],
        out_specs=_row_spec(),
        out_shape=jax.ShapeDtypeStruct((rows, D_MODEL), F32),
        compiler_params=_params(("arbitrary",), 48),
        name="final_post_norm",
    )(x_all, y, mod, post_g)


def _inproj_even_kernel(h_ref, w_ref, c_ref, s1_ref, s2_ref, o_ref, wb_ref):
    jp = pl.program_id(0)
    i = pl.program_id(1)
    n_blocks = pl.num_programs(0) - 1
    j = jp - 1
    chunk = w_ref.shape[0]

    @pl.when(jp < n_blocks)
    def _():
        wb_ref[jp % 2, pl.ds(pl.multiple_of(i * chunk, chunk), chunk), :] = w_ref[...].astype(BF16)

    @pl.when(jp == 0)
    def _():
        o_ref[...] = jnp.zeros_like(o_ref)

    def column_parts(*epilogues):
        parts = IN_TN // MXU_N
        w_cur = wb_ref.at[j % 2]
        for p in range(parts):
            cs = slice(p * MXU_N, (p + 1) * MXU_N)
            acc = jnp.dot(h_ref[...], w_cur[:, cs], preferred_element_type=F32)
            epilogues[p * len(epilogues) // parts](acc, p * MXU_N)

    def rope(acc, c0):
        c = c_ref[...]
        s1 = s1_ref[...]
        s2 = s2_ref[...]
        for hh in range(MXU_N // HEAD_DIM):
            xs = acc[:, hh * HEAD_DIM:(hh + 1) * HEAD_DIM]
            up = pltpu.roll(xs, HEAD_DIM - 32, axis=1)
            dn = pltpu.roll(xs, 32, axis=1)
            o_ref[:, c0 + hh * HEAD_DIM:c0 + (hh + 1) * HEAD_DIM] = (xs * c + up * s1 + dn * s2).astype(BF16)

    def store(fn):
        def epilogue(acc, c0):
            o_ref[:, c0:c0 + MXU_N] = fn(acc).astype(BF16)
        return epilogue

    @pl.when((j >= J_Q[0]) & (j < J_Q[1]))
    def _():
        column_parts(rope)

    @pl.when(j == J_KV)
    def _():
        column_parts(rope, store(lambda t: t))

    @pl.when(((j >= J_GA[0]) & (j < J_GA[1])) | ((j >= J_GB[0]) & (j < J_GB[1])))
    def _():
        column_parts(store(_silu))

    @pl.when((j >= J_U[0]) & (j < J_VG[1]))
    def _():
        column_parts(store(_gelu))


def _even_weight_block(j):
    return jnp.where(j < J_Q[1], j, jnp.where(j < J_KV, j + 1, J_Q[1]))


def _inproj_even(h, w_in, layer_i, rope_c, rope_s1, rope_s2, tm):
    rows = h.shape[0]
    n_blocks = AB_IN_W // IN_TN
    n_tiles = rows // tm
    chunk = D_MODEL // n_tiles
    assert rows == n_tiles * tm and D_MODEL == chunk * n_tiles and chunk % CAST_ROWS == 0
    tile = lambda jp, i: jnp.where(jp == 0, 0, i)
    tbl = pl.BlockSpec((tm, HEAD_DIM), lambda jp, i: (tile(jp, i), 0))
    return pl.pallas_call(
        _inproj_even_kernel,
        grid=(n_blocks + 1, n_tiles),
        in_specs=[
            pl.BlockSpec((tm, D_MODEL), lambda jp, i: (tile(jp, i), 0)),
            pl.BlockSpec((None, chunk, IN_TN),
                         lambda jp, i: (layer_i, i, _even_weight_block(jnp.minimum(jp, n_blocks - 1)))),
            tbl, tbl, tbl,
        ],
        out_specs=pl.BlockSpec((tm, IN_TN), lambda jp, i: (tile(jp, i), jnp.maximum(jp - 1, 0))),
        out_shape=jax.ShapeDtypeStruct((rows, AB_IN_W), BF16),
        scratch_shapes=[pltpu.VMEM((2, D_MODEL, IN_TN), BF16)],
        compiler_params=_params(("arbitrary", "arbitrary"), 58),
        name="inproj_even",
    )(h, w_in, rope_c, rope_s1, rope_s2)


def _dot_nt(a, b):
    return lax.dot_general(a, b, (((1,), (1,)), ((), ())), preferred_element_type=F32)


def _attn_kernel(q_ref, ga_ref, kp_ref, km_ref, kn_ref, vp_ref, vm_ref, vn_ref, kc_ref, vc_ref,
                 sink_ref, o_ref):
    n = pl.program_id(0)
    scale = np.float32(HEAD_DIM ** -0.5)
    grp = GQA_GROUP * BLOCK

    def group_queries(h):
        heads = [h * GQA_GROUP + g for g in range(GQA_GROUP)]
        q = jnp.concatenate([q_ref[:, hq * HEAD_DIM:(hq + 1) * HEAD_DIM] for hq in heads], axis=0)
        snk = jnp.concatenate([jnp.broadcast_to(sink_ref[hq:hq + 1, 0:1], (BLOCK, 1)) for hq in heads], axis=0)
        return q, snk

    def softmax_av(h, s, snk, v):
        m = jnp.maximum(jnp.max(s, axis=-1, keepdims=True), snk)
        p = jnp.exp(s - m)
        l = jnp.sum(p, axis=-1, keepdims=True) + jnp.exp(snk - m)
        o = jnp.dot(p.astype(BF16), v, preferred_element_type=F32) / l
        for g in range(GQA_GROUP):
            sl = slice((h * GQA_GROUP + g) * HEAD_DIM, (h * GQA_GROUP + g + 1) * HEAD_DIM)
            o_ref[:, sl] = (o[g * BLOCK:(g + 1) * BLOCK] * ga_ref[:, sl].astype(F32)).astype(BF16)

    @pl.when(n < N_BLOCKS)
    def _():
        ri = lax.broadcasted_iota(jnp.int32, (grp, BLOCK), 0) & (BLOCK - 1)
        cj = lax.broadcasted_iota(jnp.int32, (grp, BLOCK), 1)
        mask_p = (cj >= ri) & (n > 0)
        mask_n = (cj <= ri) & (n < N_BLOCKS - 1)
        for h in range(N_KV_HEADS):
            ks = slice(h * HEAD_DIM, (h + 1) * HEAD_DIM)
            k = jnp.concatenate([kp_ref[:, ks], km_ref[:, ks], kn_ref[:, ks], kc_ref[:, ks]], axis=0)
            v = jnp.concatenate([vp_ref[:, ks], vm_ref[:, ks], vn_ref[:, ks], vc_ref[:, ks]], axis=0)
            q, snk = group_queries(h)
            s = _dot_nt(q, k) * scale
            s = jnp.concatenate([jnp.where(mask_p, s[:, 0:BLOCK], NEG_INF), s[:, BLOCK:2 * BLOCK],
                                 jnp.where(mask_n, s[:, 2 * BLOCK:3 * BLOCK], NEG_INF), s[:, 3 * BLOCK:]], axis=1)
            softmax_av(h, s, snk, v)

    @pl.when(n >= N_BLOCKS)
    def _():
        for h in range(N_KV_HEADS):
            ks = slice(h * HEAD_DIM, (h + 1) * HEAD_DIM)
            q, snk = group_queries(h)
            softmax_av(h, _dot_nt(q, kc_ref[:, ks]) * scale, snk, vc_ref[:, ks])


def _attention(proj, sink_b):
    rows = proj.shape[0]
    kcol = J_KV * IN_TN // KV_W
    vcol = kcol + 1
    ctx_blk = SEQ // CTX_LEN
    lo = lambda n: jnp.clip(n - 1, 0, N_BLOCKS - 1)
    mid = lambda n: jnp.minimum(n, N_BLOCKS - 1)
    hi = lambda n: jnp.clip(n + 1, 0, N_BLOCKS - 1)
    kv = lambda f, col: pl.BlockSpec((BLOCK, KV_W), lambda n: (f(n), col))
    return pl.pallas_call(
        _attn_kernel,
        grid=(rows // BLOCK,),
        in_specs=[
            pl.BlockSpec((BLOCK, ATTN_W), lambda n: (n, J_Q[0] * IN_TN // ATTN_W)),
            pl.BlockSpec((BLOCK, ATTN_W), lambda n: (n, J_GA[0] * IN_TN // ATTN_W)),
            kv(lo, kcol), kv(mid, kcol), kv(hi, kcol),
            kv(lo, vcol), kv(mid, vcol), kv(hi, vcol),
            pl.BlockSpec((CTX_LEN, KV_W), lambda n: (ctx_blk, kcol)),
            pl.BlockSpec((CTX_LEN, KV_W), lambda n: (ctx_blk, vcol)),
            pl.BlockSpec((N_Q_HEADS, HEAD_DIM), lambda n: (0, 0)),
        ],
        out_specs=pl.BlockSpec((BLOCK, ATTN_W), lambda n: (n, 0)),
        out_shape=jax.ShapeDtypeStruct((rows, ATTN_W), BF16),
        compiler_params=_params(("arbitrary",), 40),
        name="band_attention",
    )(proj, proj, proj, proj, proj, proj, proj, proj, proj, proj, sink_b)


def _gmlp_kernel(u_ref, vg_ref, gb_ref, ws_ref, wsb_ref, lg_ref, lb_ref, o_ref):
    for ch in range(ROW_TILE // BLOCK):
        rs = slice(ch * BLOCK, (ch + 1) * BLOCK)
        v = vg_ref[rs, :].astype(F32)
        mu = jnp.mean(v, axis=-1, keepdims=True)
        vc = v - mu
        var = jnp.mean(vc * vc, axis=-1, keepdims=True)
        y = (vc * lax.rsqrt(var + LN_EPS) * lg_ref[...] + lb_ref[...]).astype(BF16)
        for g in range(GMLP_GROUPS):
            sl = slice(g * HEAD_DIM, (g + 1) * HEAD_DIM)
            s = jnp.dot(ws_ref[g].astype(BF16), y[:, sl], preferred_element_type=F32) + wsb_ref[g]
            o_ref[rs, sl] = (u_ref[rs, sl].astype(F32) * s * gb_ref[rs, sl].astype(F32)).astype(BF16)


def _gmlp(proj, ws, wsb_b, ln_g, ln_b, layer_i):
    rows = proj.shape[0]
    col = lambda c: pl.BlockSpec((ROW_TILE, GMLP_W), lambda n: (n, c))
    vec = pl.BlockSpec((None, 1, GMLP_W), lambda n: (layer_i, 0, 0))
    return pl.pallas_call(
        _gmlp_kernel,
        grid=(rows // ROW_TILE,),
        in_specs=[
            col(J_U[0] * IN_TN // GMLP_W), col(J_VG[0] * IN_TN // GMLP_W), col(J_GB[0] * IN_TN // GMLP_W),
            pl.BlockSpec((None, GMLP_GROUPS, BLOCK, BLOCK), lambda n: (layer_i, 0, 0, 0)),
            pl.BlockSpec((GMLP_GROUPS, BLOCK, BLOCK), lambda n: (0, 0, 0)),
            vec, vec,
        ],
        out_specs=pl.BlockSpec((ROW_TILE, GMLP_W), lambda n: (n, 0)),
        out_shape=jax.ShapeDtypeStruct((rows, GMLP_W), BF16),
        compiler_params=_params(("arbitrary",), 40),
        name="chunk_gmlp",
    )(proj, proj, proj, ws, wsb_b, ln_g, ln_b)


def _outproj_kernel(a1_ref, a2_ref, w_ref, y_ref, wb_ref):
    _cast_weight_once(w_ref, wb_ref)
    half = D_MODEL // 2
    for hf in range(y_ref.shape[1] // MXU_N):
        cs = slice(hf * MXU_N, (hf + 1) * MXU_N)
        acc = (jnp.dot(a1_ref[...], wb_ref[0:half, cs], preferred_element_type=F32)
               + jnp.dot(a2_ref[...], wb_ref[half:D_MODEL, cs], preferred_element_type=F32))
        y_ref[:, cs] = acc.astype(BF16)


def _outproj(a1, a2, a2_col, w, layer_i, rows, tm):
    tn = IN_TN
    half = D_MODEL // 2
    return pl.pallas_call(
        _outproj_kernel,
        grid=(D_MODEL // tn, rows // tm),
        in_specs=[
            pl.BlockSpec((tm, half), lambda j, i: (i, 0)),
            pl.BlockSpec((tm, half), lambda j, i: (i, a2_col)),
            pl.BlockSpec((None, D_MODEL, tn), lambda j, i: (layer_i, 0, j)),
        ],
        out_specs=pl.BlockSpec((tm, tn), lambda j, i: (i, j)),
        out_shape=jax.ShapeDtypeStruct((rows, D_MODEL), BF16),
        scratch_shapes=[pltpu.VMEM((D_MODEL, tn), BF16)],
        compiler_params=_params(("arbitrary", "arbitrary"), 58),
        name="outproj",
    )(a1, a2, w)


def _inproj_conv_kernel(h_ref, wa_ref, wb_ref, wg_ref, glu_ref, sg_ref, wab_ref, wbb_ref, wgb_ref):
    _cast_weight_once(wa_ref, wab_ref)
    _cast_weight_once(wb_ref, wbb_ref)
    _cast_weight_once(wg_ref, wgb_ref)
    h = h_ref[...]
    a = jnp.dot(h, wab_ref[...], preferred_element_type=F32)
    b = jnp.dot(h, wbb_ref[...], preferred_element_type=F32)
    glu_ref[...] = (a * jax.nn.sigmoid(b)).astype(BF16)
    g = jnp.dot(h, wgb_ref[...], preferred_element_type=F32)
    sg_ref[...] = _silu(g).astype(BF16)


def _inproj_conv(h, w, layer_i, tm):
    rows = h.shape[0]
    tn = MXU_N
    nj = D_MODEL // tn
    wspec = lambda part: pl.BlockSpec((None, D_MODEL, tn), lambda j, i: (layer_i, 0, j + part * nj))
    out = pl.BlockSpec((tm, tn), lambda j, i: (i, j))
    return pl.pallas_call(
        _inproj_conv_kernel,
        grid=(nj, rows // tm),
        in_specs=[pl.BlockSpec((tm, D_MODEL), lambda j, i: (i, 0)), wspec(0), wspec(1), wspec(2)],
        out_specs=[out, out],
        out_shape=[jax.ShapeDtypeStruct((rows, D_MODEL), BF16)] * 2,
        scratch_shapes=[pltpu.VMEM((D_MODEL, tn), BF16)] * 3,
        compiler_params=_params(("arbitrary", "arbitrary"), 60),
        name="inproj_conv",
    )(h, w, w, w)


def _conv_kernel(gp_ref, gm_ref, gn_ref, sg_ref, dw_ref, dwb_ref, lg_ref, lb_ref, o_ref, y_ref, s1_ref, sh_ref):
    i = pl.program_id(0)
    tr = ROW_TILE
    first_ctx = SEQ // tr
    has_prev = (i != 0) & (i != first_ctx)
    has_next = (i != first_ctx - 1) & (i != pl.num_programs(0) - 1)
    base = CONV_HALO - CONV_K // 2
    sub = 128
    s1_ref[...] = jnp.zeros_like(s1_ref)

    def lane_block(c, carry):
        ls = pl.ds(pl.multiple_of(c * LANES, LANES), LANES)
        prev = jnp.where(has_prev, gp_ref[:, ls].astype(F32), 0.0)
        nxt = jnp.where(has_next, gn_ref[:, ls].astype(F32), 0.0)
        window = jnp.concatenate([prev, gm_ref[:, ls].astype(F32), nxt], axis=0)
        sh_ref[0] = window
        for s in range(1, SUBLANES):
            sh_ref[s] = pltpu.roll(window, window.shape[0] - s, axis=0)
        for r0 in range(0, tr, sub):
            acc = jnp.broadcast_to(dwb_ref[:, ls], (sub, LANES))
            for d in range(CONV_K):
                q, s = divmod(base + d, SUBLANES)
                acc = acc + sh_ref[s, r0 + q * SUBLANES:r0 + q * SUBLANES + sub, :] * dw_ref[d:d + 1, ls]
            y_ref[r0:r0 + sub, ls] = acc
            s1_ref[r0:r0 + sub, :] += acc
        return carry

    lax.fori_loop(0, D_MODEL // LANES, lane_block, 0)

    def norm_rows(r, carry):
        rs = pl.ds(pl.multiple_of(r * NORM_ROWS, NORM_ROWS), NORM_ROWS)
        mu = jnp.sum(s1_ref[rs, :], axis=-1, keepdims=True) * np.float32(1.0 / D_MODEL)
        yc = y_ref[rs, :] - mu
        var = jnp.mean(yc * yc, axis=-1, keepdims=True)
        z = yc * lax.rsqrt(var + LN_EPS) * lg_ref[...] + lb_ref[...]
        o_ref[rs, :] = (_silu(z) * sg_ref[rs, :].astype(F32)).astype(BF16)
        return carry

    lax.fori_loop(0, tr // NORM_ROWS, norm_rows, 0, unroll=4)


def _conv_mix(glu, sg, dw, dw_b, ln_g, ln_b, layer_i, rows):
    tr = ROW_TILE
    per = tr // CONV_HALO
    n_halo = glu.shape[0] // CONV_HALO
    vec = pl.BlockSpec((None, 1, D_MODEL), lambda i: (layer_i, 0, 0))
    return pl.pallas_call(
        _conv_kernel,
        grid=(rows // tr,),
        in_specs=[
            pl.BlockSpec((CONV_HALO, D_MODEL), lambda i: (jnp.maximum(i * per - 1, 0), 0)),
            pl.BlockSpec((tr, D_MODEL), lambda i: (i, 0)),
            pl.BlockSpec((CONV_HALO, D_MODEL), lambda i: (jnp.minimum((i + 1) * per, n_halo - 1), 0)),
            pl.BlockSpec((tr, D_MODEL), lambda i: (i, 0)),
            pl.BlockSpec((None, CONV_K, D_MODEL), lambda i: (layer_i, 0, 0)),
            vec, vec, vec,
        ],
        out_specs=pl.BlockSpec((tr, D_MODEL), lambda i: (i, 0)),
        out_shape=jax.ShapeDtypeStruct((rows, D_MODEL), BF16),
        scratch_shapes=[pltpu.VMEM((tr, D_MODEL), F32), pltpu.VMEM((tr, LANES), F32),
                        pltpu.VMEM((SUBLANES, tr + 2 * CONV_HALO, LANES), F32)],
        compiler_params=_params(("arbitrary",), 48),
        name="conv_ln_swish",
    )(glu, glu, glu, sg, dw, dw_b, ln_g, ln_b)


def _rope_tables():
    quarter = HEAD_DIM // 4
    inv = (1.0 / (ROPE_BASE ** (np.arange(quarter, dtype=np.float32) / quarter))).astype(np.float32)
    pos = np.arange(SEQ, dtype=np.int32)
    row = (pos // GRID_W).astype(np.float32)[:, None] * inv[None, :]
    col = (pos % GRID_W).astype(np.float32)[:, None] * inv[None, :]
    zero = np.zeros_like(row)
    cos = np.concatenate([np.cos(row), np.cos(row), np.cos(col), np.cos(col)], axis=-1)
    s_up = np.concatenate([-np.sin(row), zero, -np.sin(col), zero], axis=-1)
    s_dn = np.concatenate([zero, np.sin(row), zero, np.sin(col)], axis=-1)
    pad = lambda t, v: np.concatenate([t, np.full((CTX_LEN, HEAD_DIM), v, np.float32)], axis=0).astype(np.float32)
    return pad(cos, 1.0), pad(s_up, 0.0), pad(s_dn, 0.0)


def kernel(x, c, ctx, c_ctx, ada_w, ada_b, pre_g, post_g, ab_w_in, ab_sink, ab_ln_g, ab_ln_b, ab_ws, ab_ws_b,
           ab_w_out, cv_w_in, cv_dw, cv_dw_b, cv_ln_g, cv_ln_b, cv_w_out):
    assert x.shape == (1, SEQ, D_MODEL) and ctx.shape == (1, CTX_LEN, D_MODEL)
    cc = jnp.concatenate([c, c_ctx[None, :], jnp.zeros((6, D_MODEL), F32)], axis=0)
    mod = _modulation(cc, ada_w, ada_b)
    rope_c, rope_s1, rope_s2 = _rope_tables()
    pre_g = pre_g.reshape(DEPTH, 1, D_MODEL)
    post_g = post_g.reshape(DEPTH, 1, D_MODEL)
    vec3 = lambda t: t.reshape(t.shape[0], 1, t.shape[1])

    h = _first_prenorm(x, ctx, mod, pre_g)
    xs = (x, ctx)
    y = None
    for layer in range(DEPTH):
        last = layer == DEPTH - 1
        i = layer // 2
        rows = SEQ if last else ROWS
        tm = TM_LATENT if last else TM_ALL
        tm_wide = TM_WIDE_LATENT if last else TM_WIDE_ALL
        if layer > 0:
            x_all, h = _post_pre(xs, y, mod, post_g, pre_g, layer, rows)
            xs = (x_all,)
        if layer % 2 == 0:
            proj = _inproj_even(h, ab_w_in, i, rope_c, rope_s1, rope_s2, tm)
            sink_b = jnp.broadcast_to(ab_sink[i][:, None], (N_Q_HEADS, HEAD_DIM))
            attn = _attention(proj, sink_b)
            wsb_b = jnp.broadcast_to(ab_ws_b[i][:, :, None], (GMLP_GROUPS, BLOCK, BLOCK))
            mix = _gmlp(proj, ab_ws, wsb_b, vec3(ab_ln_g), vec3(ab_ln_b), i)
            y = _outproj(attn, mix, 0, ab_w_out, i, rows, tm_wide)
        else:
            glu, sg = _inproj_conv(h, cv_w_in, i, tm)
            z = _conv_mix(glu, sg, cv_dw, vec3(cv_dw_b), vec3(cv_ln_g), vec3(cv_ln_b), i, rows)
            y = _outproj(z, z, 1, cv_w_out, i, rows, tm_wide)
    out = _final_post(x_all, y, mod, post_g, DEPTH - 1, SEQ)
    return out[None]
```

```python
import functools

import jax
import jax.numpy as jnp
import numpy as np
from jax import lax
from jax.experimental import pallas as pl
from jax.experimental.pallas import tpu as pltpu

D_MODEL = 4096
SEQ = 8192
DEPTH = 4
GRID_W = 64
CTX_LEN = 256
ROWS = SEQ + CTX_LEN
HEAD_DIM = 128
N_Q_HEADS = 16
N_KV_HEADS = 4
GQA_GROUP = 4
ATTN_W = 2048
KV_W = 512
BLOCK = 128
N_BLOCKS = SEQ // BLOCK
GMLP_GROUPS = 16
GMLP_W = 2048
AB_IN_W = 11264
CONV_K = 31
CONV_HALO = 16
ROPE_BASE = 10000.0
RMS_EPS = 1e-6
LN_EPS = 1e-5
NEG_INF = -1e30

F32 = jnp.float32
BF16 = jnp.bfloat16
MIB = 1024 * 1024
LANES = 128
SUBLANES = 8
MXU_N = 256

ROW_TILE = 256
CAST_ROWS = 256
TM_ALL = 1056
TM_LATENT = 1024
TM_WIDE_ALL = 528
TM_WIDE_LATENT = 512
NORM_ROWS = 16

IN_TN = 1024
J_Q = (0, 2)
J_GA = (2, 4)
J_U = (4, 6)
J_VG = (6, 8)
J_GB = (8, 10)
J_KV = 10


def _params(semantics, vmem_mib):
    return pltpu.CompilerParams(dimension_semantics=semantics, vmem_limit_bytes=vmem_mib * MIB)


def _silu(t):
    return t * jax.nn.sigmoid(t)


def _gelu(t):
    return 0.5 * t * (1.0 + lax.erf(t * np.float32(np.sqrt(0.5))))


def _rms(t):
    return t * lax.rsqrt(jnp.mean(t * t, axis=-1, keepdims=True) + RMS_EPS)


def _cast_weight_once(w_ref, wb_ref):
    @pl.when(pl.program_id(1) == 0)
    def _():
        def rows(r, carry):
            rs = pl.ds(pl.multiple_of(r * CAST_ROWS, CAST_ROWS), CAST_ROWS)
            wb_ref[rs, :] = w_ref[rs, :].astype(BF16)
            return carry
        lax.fori_loop(0, w_ref.shape[0] // CAST_ROWS, rows, 0)


def _mod_kernel(cc_ref, w_ref, b_ref, o_ref):
    a = _silu(cc_ref[...]).astype(BF16)
    w = w_ref[...].astype(BF16)
    o_ref[...] = jnp.dot(a, w, preferred_element_type=F32) + b_ref[...]


def _modulation(cc, ada_w, ada_b):
    tn = 512
    n3 = 3 * D_MODEL
    return pl.pallas_call(
        _mod_kernel,
        grid=(DEPTH, n3 // tn),
        in_specs=[
            pl.BlockSpec((8, D_MODEL), lambda l, j: (0, 0)),
            pl.BlockSpec((None, D_MODEL, tn), lambda l, j: (l, 0, j)),
            pl.BlockSpec((None, 1, tn), lambda l, j: (l, 0, j)),
        ],
        out_specs=pl.BlockSpec((None, 8, tn), lambda l, j: (l, 0, j)),
        out_shape=jax.ShapeDtypeStruct((DEPTH, 8, n3), F32),
        compiler_params=_params(("arbitrary", "arbitrary"), 40),
        name="adaln_mod",
    )(cc, ada_w, ada_b.reshape(DEPTH, 1, n3))


def _modulate(x, mod_ref, g_ref, is_ctx):
    shift = jnp.where(is_ctx, mod_ref[1:2, 0:D_MODEL], mod_ref[0:1, 0:D_MODEL])
    scale = jnp.where(is_ctx, mod_ref[1:2, D_MODEL:2 * D_MODEL], mod_ref[0:1, D_MODEL:2 * D_MODEL])
    return (_rms(x) * g_ref[...] * (1.0 + scale) + shift).astype(BF16)


def _residual(x, y_ref, mod_ref, pg_ref, is_ctx):
    gate = jnp.where(is_ctx, mod_ref[1:2, 2 * D_MODEL:3 * D_MODEL], mod_ref[0:1, 2 * D_MODEL:3 * D_MODEL])
    return x + gate * (_rms(y_ref[...].astype(F32)) * pg_ref[...])


def _first_prenorm_kernel(x_ref, ctx_ref, mod_ref, g_ref, h_ref):
    is_ctx = pl.program_id(0) * ROW_TILE >= SEQ
    x = jnp.where(is_ctx, ctx_ref[...], x_ref[...])
    h_ref[...] = _modulate(x, mod_ref, g_ref, is_ctx)


def _post_pre_kernel(x_ref, y_ref, modp_ref, pg_ref, mod_ref, g_ref, xo_ref, h_ref):
    is_ctx = pl.program_id(0) * ROW_TILE >= SEQ
    x = _residual(x_ref[...], y_ref, modp_ref, pg_ref, is_ctx)
    xo_ref[...] = x
    h_ref[...] = _modulate(x, mod_ref, g_ref, is_ctx)


def _post_pre_split_kernel(x_ref, ctx_ref, y_ref, modp_ref, pg_ref, mod_ref, g_ref, xo_ref, h_ref):
    is_ctx = pl.program_id(0) * ROW_TILE >= SEQ
    x = _residual(jnp.where(is_ctx, ctx_ref[...], x_ref[...]), y_ref, modp_ref, pg_ref, is_ctx)
    xo_ref[...] = x
    h_ref[...] = _modulate(x, mod_ref, g_ref, is_ctx)


def _final_post_kernel(x_ref, y_ref, modp_ref, pg_ref, xo_ref):
    xo_ref[...] = _residual(x_ref[...], y_ref, modp_ref, pg_ref, False)


def _row_spec():
    return pl.BlockSpec((ROW_TILE, D_MODEL), lambda i: (i, 0))


def _mod_spec(layer):
    return pl.BlockSpec((None, 8, 3 * D_MODEL), lambda i: (layer, 0, 0))


def _gain_spec(layer):
    return pl.BlockSpec((None, 1, D_MODEL), lambda i: (layer, 0, 0))


def _split_input_specs():
    last_x = SEQ // ROW_TILE - 1
    return [pl.BlockSpec((None, ROW_TILE, D_MODEL), lambda i: (0, jnp.minimum(i, last_x), 0)),
            pl.BlockSpec((None, ROW_TILE, D_MODEL), lambda i: (0, jnp.maximum(i - last_x - 1, 0), 0))]


def _first_prenorm(x, ctx, mod, pre_g):
    return pl.pallas_call(
        _first_prenorm_kernel,
        grid=(ROWS // ROW_TILE,),
        in_specs=_split_input_specs() + [_mod_spec(0), _gain_spec(0)],
        out_specs=_row_spec(),
        out_shape=jax.ShapeDtypeStruct((ROWS, D_MODEL), BF16),
        compiler_params=_params(("arbitrary",), 48),
        name="first_prenorm",
    )(x, ctx, mod, pre_g)


def _post_pre(xs, y, mod, post_g, pre_g, layer, rows):
    split = len(xs) == 2
    return pl.pallas_call(
        _post_pre_split_kernel if split else _post_pre_kernel,
        grid=(rows // ROW_TILE,),
        in_specs=(_split_input_specs() if split else [_row_spec()])
        + [_row_spec(), _mod_spec(layer - 1), _gain_spec(layer - 1), _mod_spec(layer), _gain_spec(layer)],
        out_specs=[_row_spec(), _row_spec()],
        out_shape=[jax.ShapeDtypeStruct((rows, D_MODEL), F32), jax.ShapeDtypeStruct((rows, D_MODEL), BF16)],
        compiler_params=_params(("arbitrary",), 48),
        name="post_pre_norm",
    )(*xs, y, mod, post_g, mod, pre_g)


def _final_post(x_all, y, mod, post_g, layer, rows):
    return pl.pallas_call(
        _final_post_kernel,
        grid=(rows // ROW_TILE,),
        in_specs=[_row_spec(), _row_spec(), _mod_spec(layer), _gain_spec(layer)],
        out_specs=_row_spec(),
        out_shape=jax.ShapeDtypeStruct((rows, D_MODEL), F32),
        compiler_params=_params(("arbitrary",), 48),
        name="final_post_norm",
    )(x_all, y, mod, post_g)


def _inproj_even_kernel(h_ref, w_ref, c_ref, s1_ref, s2_ref, o_ref, wb_ref):
    jp = pl.program_id(0)
    i = pl.program_id(1)
    n_blocks = pl.num_programs(0) - 1
    j = jp - 1
    chunk = w_ref.shape[0]

    @pl.when(jp < n_blocks)
    def _():
        wb_ref[jp % 2, pl.ds(pl.multiple_of(i * chunk, chunk), chunk), :] = w_ref[...].astype(BF16)

    @pl.when(jp == 0)
    def _():
        o_ref[...] = jnp.zeros_like(o_ref)

    def column_parts(*epilogues):
        parts = IN_TN // MXU_N
        w_cur = wb_ref.at[j % 2]
        for p in range(parts):
            cs = slice(p * MXU_N, (p + 1) * MXU_N)
            acc = jnp.dot(h_ref[...], w_cur[:, cs], preferred_element_type=F32)
            epilogues[p * len(epilogues) // parts](acc, p * MXU_N)

    def rope(acc, c0):
        c = c_ref[...]
        s1 = s1_ref[...]
        s2 = s2_ref[...]
        for hh in range(MXU_N // HEAD_DIM):
            xs = acc[:, hh * HEAD_DIM:(hh + 1) * HEAD_DIM]
            up = pltpu.roll(xs, HEAD_DIM - 32, axis=1)
            dn = pltpu.roll(xs, 32, axis=1)
            o_ref[:, c0 + hh * HEAD_DIM:c0 + (hh + 1) * HEAD_DIM] = (xs * c + up * s1 + dn * s2).astype(BF16)

    def store(fn):
        def epilogue(acc, c0):
            o_ref[:, c0:c0 + MXU_N] = fn(acc).astype(BF16)
        return epilogue

    @pl.when((j >= J_Q[0]) & (j < J_Q[1]))
    def _():
        column_parts(rope)

    @pl.when(j == J_KV)
    def _():
        column_parts(rope, store(lambda t: t))

    @pl.when(((j >= J_GA[0]) & (j < J_GA[1])) | ((j >= J_GB[0]) & (j < J_GB[1])))
    def _():
        column_parts(store(_silu))

    @pl.when((j >= J_U[0]) & (j < J_VG[1]))
    def _():
        column_parts(store(_gelu))


def _even_weight_block(j):
    return jnp.where(j < J_Q[1], j, jnp.where(j < J_KV, j + 1, J_Q[1]))


def _inproj_even(h, w_in, layer_i, rope_c, rope_s1, rope_s2, tm):
    rows = h.shape[0]
    n_blocks = AB_IN_W // IN_TN
    n_tiles = rows // tm
    chunk = D_MODEL // n_tiles
    assert rows == n_tiles * tm and D_MODEL == chunk * n_tiles and chunk % CAST_ROWS == 0
    tile = lambda jp, i: jnp.where(jp == 0, 0, i)
    tbl = pl.BlockSpec((tm, HEAD_DIM), lambda jp, i: (tile(jp, i), 0))
    return pl.pallas_call(
        _inproj_even_kernel,
        grid=(n_blocks + 1, n_tiles),
        in_specs=[
            pl.BlockSpec((tm, D_MODEL), lambda jp, i: (tile(jp, i), 0)),
            pl.BlockSpec((None, chunk, IN_TN),
                         lambda jp, i: (layer_i, i, _even_weight_block(jnp.minimum(jp, n_blocks - 1)))),
            tbl, tbl, tbl,
        ],
        out_specs=pl.BlockSpec((tm, IN_TN), lambda jp, i: (tile(jp, i), jnp.maximum(jp - 1, 0))),
        out_shape=jax.ShapeDtypeStruct((rows, AB_IN_W), BF16),
        scratch_shapes=[pltpu.VMEM((2, D_MODEL, IN_TN), BF16)],
        compiler_params=_params(("arbitrary", "arbitrary"), 58),
        name="inproj_even",
    )(h, w_in, rope_c, rope_s1, rope_s2)


def _dot_nt(a, b):
    return lax.dot_general(a, b, (((1,), (1,)), ((), ())), preferred_element_type=F32)


def _attn_kernel(q_ref, ga_ref, kp_ref, km_ref, kn_ref, vp_ref, vm_ref, vn_ref, kc_ref, vc_ref,
                 sink_ref, o_ref):
    n = pl.program_id(0)
    scale = np.float32(HEAD_DIM ** -0.5)
    grp = GQA_GROUP * BLOCK

    def group_queries(h):
        heads = [h * GQA_GROUP + g for g in range(GQA_GROUP)]
        q = jnp.concatenate([q_ref[:, hq * HEAD_DIM:(hq + 1) * HEAD_DIM] for hq in heads], axis=0)
        snk = jnp.concatenate([jnp.broadcast_to(sink_ref[hq:hq + 1, 0:1], (BLOCK, 1)) for hq in heads], axis=0)
        return q, snk

    def softmax_av(h, s, snk, v):
        m = jnp.maximum(jnp.max(s, axis=-1, keepdims=True), snk)
        p = jnp.exp(s - m)
        l = jnp.sum(p, axis=-1, keepdims=True) + jnp.exp(snk - m)
        o = jnp.dot(p.astype(BF16), v, preferred_element_type=F32) / l
        for g in range(GQA_GROUP):
            sl = slice((h * GQA_GROUP + g) * HEAD_DIM, (h * GQA_GROUP + g + 1) * HEAD_DIM)
            o_ref[:, sl] = (o[g * BLOCK:(g + 1) * BLOCK] * ga_ref[:, sl].astype(F32)).astype(BF16)

    @pl.when(n < N_BLOCKS)
    def _():
        ri = lax.broadcasted_iota(jnp.int32, (grp, BLOCK), 0) & (BLOCK - 1)
        cj = lax.broadcasted_iota(jnp.int32, (grp, BLOCK), 1)
        mask_p = (cj >= ri) & (n > 0)
        mask_n = (cj <= ri) & (n < N_BLOCKS - 1)
        for h in range(N_KV_HEADS):
            ks = slice(h * HEAD_DIM, (h + 1) * HEAD_DIM)
            k = jnp.concatenate([kp_ref[:, ks], km_ref[:, ks], kn_ref[:, ks], kc_ref[:, ks]], axis=0)
            v = jnp.concatenate([vp_ref[:, ks], vm_ref[:, ks], vn_ref[:, ks], vc_ref[:, ks]], axis=0)
            q, snk = group_queries(h)
            s = _dot_nt(q, k) * scale
            s = jnp.concatenate([jnp.where(mask_p, s[:, 0:BLOCK], NEG_INF), s[:, BLOCK:2 * BLOCK],
                                 jnp.where(mask_n, s[:, 2 * BLOCK:3 * BLOCK], NEG_INF), s[:, 3 * BLOCK:]], axis=1)
            softmax_av(h, s, snk, v)

    @pl.when(n >= N_BLOCKS)
    def _():
        for h in range(N_KV_HEADS):
            ks = slice(h * HEAD_DIM, (h + 1) * HEAD_DIM)
            q, snk = group_queries(h)
            softmax_av(h, _dot_nt(q, kc_ref[:, ks]) * scale, snk, vc_ref[:, ks])


def _attention(proj, sink_b):
    rows = proj.shape[0]
    kcol = J_KV * IN_TN // KV_W
    vcol = kcol + 1
    ctx_blk = SEQ // CTX_LEN
    lo = lambda n: jnp.clip(n - 1, 0, N_BLOCKS - 1)
    mid = lambda n: jnp.minimum(n, N_BLOCKS - 1)
    hi = lambda n: jnp.clip(n + 1, 0, N_BLOCKS - 1)
    kv = lambda f, col: pl.BlockSpec((BLOCK, KV_W), lambda n: (f(n), col))
    return pl.pallas_call(
        _attn_kernel,
        grid=(rows // BLOCK,),
        in_specs=[
            pl.BlockSpec((BLOCK, ATTN_W), lambda n: (n, J_Q[0] * IN_TN // ATTN_W)),
            pl.BlockSpec((BLOCK, ATTN_W), lambda n: (n, J_GA[0] * IN_TN // ATTN_W)),
            kv(lo, kcol), kv(mid, kcol), kv(hi, kcol),
            kv(lo, vcol), kv(mid, vcol), kv(hi, vcol),
            pl.BlockSpec((CTX_LEN, KV_W), lambda n: (ctx_blk, kcol)),
            pl.BlockSpec((CTX_LEN, KV_W), lambda n: (ctx_blk, vcol)),
            pl.BlockSpec((N_Q_HEADS, HEAD_DIM), lambda n: (0, 0)),
        ],
        out_specs=pl.BlockSpec((BLOCK, ATTN_W), lambda n: (n, 0)),
        out_shape=jax.ShapeDtypeStruct((rows, ATTN_W), BF16),
        compiler_params=_params(("arbitrary",), 40),
        name="band_attention",
    )(proj, proj, proj, proj, proj, proj, proj, proj, proj, proj, sink_b)


def _gmlp_kernel(u_ref, vg_ref, gb_ref, ws_ref, wsb_ref, lg_ref, lb_ref, o_ref):
    for ch in range(ROW_TILE // BLOCK):
        rs = slice(ch * BLOCK, (ch + 1) * BLOCK)
        v = vg_ref[rs, :].astype(F32)
        mu = jnp.mean(v, axis=-1, keepdims=True)
        vc = v - mu
        var = jnp.mean(vc * vc, axis=-1, keepdims=True)
        y = (vc * lax.rsqrt(var + LN_EPS) * lg_ref[...] + lb_ref[...]).astype(BF16)
        for g in range(GMLP_GROUPS):
            sl = slice(g * HEAD_DIM, (g + 1) * HEAD_DIM)
            s = jnp.dot(ws_ref[g].astype(BF16), y[:, sl], preferred_element_type=F32) + wsb_ref[g]
            o_ref[rs, sl] = (u_ref[rs, sl].astype(F32) * s * gb_ref[rs, sl].astype(F32)).astype(BF16)


def _gmlp(proj, ws, wsb_b, ln_g, ln_b, layer_i):
    rows = proj.shape[0]
    col = lambda c: pl.BlockSpec((ROW_TILE, GMLP_W), lambda n: (n, c))
    vec = pl.BlockSpec((None, 1, GMLP_W), lambda n: (layer_i, 0, 0))
    return pl.pallas_call(
        _gmlp_kernel,
        grid=(rows // ROW_TILE,),
        in_specs=[
            col(J_U[0] * IN_TN // GMLP_W), col(J_VG[0] * IN_TN // GMLP_W), col(J_GB[0] * IN_TN // GMLP_W),
            pl.BlockSpec((None, GMLP_GROUPS, BLOCK, BLOCK), lambda n: (layer_i, 0, 0, 0)),
            pl.BlockSpec((GMLP_GROUPS, BLOCK, BLOCK), lambda n: (0, 0, 0)),
            vec, vec,
        ],
        out_specs=pl.BlockSpec((ROW_TILE, GMLP_W), lambda n: (n, 0)),
        out_shape=jax.ShapeDtypeStruct((rows, GMLP_W), BF16),
        compiler_params=_params(("arbitrary",), 40),
        name="chunk_gmlp",
    )(proj, proj, proj, ws, wsb_b, ln_g, ln_b)


def _outproj_kernel(a1_ref, a2_ref, w_ref, y_ref, wb_ref):
    _cast_weight_once(w_ref, wb_ref)
    half = D_MODEL // 2
    for hf in range(y_ref.shape[1] // MXU_N):
        cs = slice(hf * MXU_N, (hf + 1) * MXU_N)
        acc = (jnp.dot(a1_ref[...], wb_ref[0:half, cs], preferred_element_type=F32)
               + jnp.dot(a2_ref[...], wb_ref[half:D_MODEL, cs], preferred_element_type=F32))
        y_ref[:, cs] = acc.astype(BF16)


def _outproj(a1, a2, a2_col, w, layer_i, rows, tm):
    tn = IN_TN
    half = D_MODEL // 2
    return pl.pallas_call(
        _outproj_kernel,
        grid=(D_MODEL // tn, rows // tm),
        in_specs=[
            pl.BlockSpec((tm, half), lambda j, i: (i, 0)),
            pl.BlockSpec((tm, half), lambda j, i: (i, a2_col)),
            pl.BlockSpec((None, D_MODEL, tn), lambda j, i: (layer_i, 0, j)),
        ],
        out_specs=pl.BlockSpec((tm, tn), lambda j, i: (i, j)),
        out_shape=jax.ShapeDtypeStruct((rows, D_MODEL), BF16),
        scratch_shapes=[pltpu.VMEM((D_MODEL, tn), BF16)],
        compiler_params=_params(("arbitrary", "arbitrary"), 58),
        name="outproj",
    )(a1, a2, w)


def _inproj_conv_kernel(h_ref, wa_ref, wb_ref, wg_ref, glu_ref, sg_ref, wab_ref, wbb_ref, wgb_ref):
    jp = pl.program_id(0)
    i = pl.program_id(1)
    n_blocks = pl.num_programs(0) - 1
    chunk = wa_ref.shape[0]

    @pl.when(jp < n_blocks)
    def _():
        rs = pl.ds(pl.multiple_of(i * chunk, chunk), chunk)
        for src, dst in ((wa_ref, wab_ref), (wb_ref, wbb_ref), (wg_ref, wgb_ref)):
            dst[jp % 2, rs, :] = src[...].astype(BF16)

    @pl.when(jp == 0)
    def _():
        glu_ref[...] = jnp.zeros_like(glu_ref)
        sg_ref[...] = jnp.zeros_like(sg_ref)

    @pl.when(jp >= 1)
    def _():
        slot = (jp - 1) % 2
        h = h_ref[...]
        a = jnp.dot(h, wab_ref[slot], preferred_element_type=F32)
        b = jnp.dot(h, wbb_ref[slot], preferred_element_type=F32)
        glu_ref[...] = (a * jax.nn.sigmoid(b)).astype(BF16)
        g = jnp.dot(h, wgb_ref[slot], preferred_element_type=F32)
        sg_ref[...] = _silu(g).astype(BF16)


def _inproj_conv(h, w, layer_i, tm):
    rows = h.shape[0]
    tn = MXU_N
    nj = D_MODEL // tn
    n_tiles = rows // tm
    chunk = D_MODEL // n_tiles
    assert rows == n_tiles * tm and D_MODEL == chunk * n_tiles and chunk % CAST_ROWS == 0
    tile = lambda jp, i: jnp.where(jp == 0, 0, i)
    wspec = lambda part: pl.BlockSpec(
        (None, chunk, tn), lambda jp, i: (layer_i, i, jnp.minimum(jp, nj - 1) + part * nj))
    out = pl.BlockSpec((tm, tn), lambda jp, i: (tile(jp, i), jnp.maximum(jp - 1, 0)))
    return pl.pallas_call(
        _inproj_conv_kernel,
        grid=(nj + 1, n_tiles),
        in_specs=[pl.BlockSpec((tm, D_MODEL), lambda jp, i: (tile(jp, i), 0)), wspec(0), wspec(1), wspec(2)],
        out_specs=[out, out],
        out_shape=[jax.ShapeDtypeStruct((rows, D_MODEL), BF16)] * 2,
        scratch_shapes=[pltpu.VMEM((2, D_MODEL, tn), BF16)] * 3,
        compiler_params=_params(("arbitrary", "arbitrary"), 60),
        name="inproj_conv",
    )(h, w, w, w)


def _conv_kernel(gp_ref, gm_ref, gn_ref, sg_ref, dw_ref, dwb_ref, lg_ref, lb_ref, o_ref, y_ref, s1_ref, sh_ref):
    i = pl.program_id(0)
    tr = ROW_TILE
    first_ctx = SEQ // tr
    has_prev = (i != 0) & (i != first_ctx)
    has_next = (i != first_ctx - 1) & (i != pl.num_programs(0) - 1)
    base = CONV_HALO - CONV_K // 2
    sub = 128
    s1_ref[...] = jnp.zeros_like(s1_ref)

    def lane_block(c, carry):
        ls = pl.ds(pl.multiple_of(c * LANES, LANES), LANES)
        prev = jnp.where(has_prev, gp_ref[:, ls].astype(F32), 0.0)
        nxt = jnp.where(has_next, gn_ref[:, ls].astype(F32), 0.0)
        window = jnp.concatenate([prev, gm_ref[:, ls].astype(F32), nxt], axis=0)
        sh_ref[0] = window
        for s in range(1, SUBLANES):
            sh_ref[s] = pltpu.roll(window, window.shape[0] - s, axis=0)
        for r0 in range(0, tr, sub):
            acc = jnp.broadcast_to(dwb_ref[:, ls], (sub, LANES))
            for d in range(CONV_K):
                q, s = divmod(base + d, SUBLANES)
                acc = acc + sh_ref[s, r0 + q * SUBLANES:r0 + q * SUBLANES + sub, :] * dw_ref[d:d + 1, ls]
            y_ref[r0:r0 + sub, ls] = acc
            s1_ref[r0:r0 + sub, :] += acc
        return carry

    lax.fori_loop(0, D_MODEL // LANES, lane_block, 0)

    def norm_rows(r, carry):
        rs = pl.ds(pl.multiple_of(r * NORM_ROWS, NORM_ROWS), NORM_ROWS)
        mu = jnp.sum(s1_ref[rs, :], axis=-1, keepdims=True) * np.float32(1.0 / D_MODEL)
        yc = y_ref[rs, :] - mu
        var = jnp.mean(yc * yc, axis=-1, keepdims=True)
        z = yc * lax.rsqrt(var + LN_EPS) * lg_ref[...] + lb_ref[...]
        o_ref[rs, :] = (_silu(z) * sg_ref[rs, :].astype(F32)).astype(BF16)
        return carry

    lax.fori_loop(0, tr // NORM_ROWS, norm_rows, 0, unroll=4)


def _conv_mix(glu, sg, dw, dw_b, ln_g, ln_b, layer_i, rows):
    tr = ROW_TILE
    per = tr // CONV_HALO
    n_halo = glu.shape[0] // CONV_HALO
    vec = pl.BlockSpec((None, 1, D_MODEL), lambda i: (layer_i, 0, 0))
    return pl.pallas_call(
        _conv_kernel,
        grid=(rows // tr,),
        in_specs=[
            pl.BlockSpec((CONV_HALO, D_MODEL), lambda i: (jnp.maximum(i * per - 1, 0), 0)),
            pl.BlockSpec((tr, D_MODEL), lambda i: (i, 0)),
            pl.BlockSpec((CONV_HALO, D_MODEL), lambda i: (jnp.minimum((i + 1) * per, n_halo - 1), 0)),
            pl.BlockSpec((tr, D_MODEL), lambda i: (i, 0)),
            pl.BlockSpec((None, CONV_K, D_MODEL), lambda i: (layer_i, 0, 0)),
            vec, vec, vec,
        ],
        out_specs=pl.BlockSpec((tr, D_MODEL), lambda i: (i, 0)),
        out_shape=jax.ShapeDtypeStruct((rows, D_MODEL), BF16),
        scratch_shapes=[pltpu.VMEM((tr, D_MODEL), F32), pltpu.VMEM((tr, LANES), F32),
                        pltpu.VMEM((SUBLANES, tr + 2 * CONV_HALO, LANES), F32)],
        compiler_params=_params(("arbitrary",), 48),
        name="conv_ln_swish",
    )(glu, glu, glu, sg, dw, dw_b, ln_g, ln_b)


def _rope_tables():
    quarter = HEAD_DIM // 4
    inv = (1.0 / (ROPE_BASE ** (np.arange(quarter, dtype=np.float32) / quarter))).astype(np.float32)
    pos = np.arange(SEQ, dtype=np.int32)
    row = (pos // GRID_W).astype(np.float32)[:, None] * inv[None, :]
    col = (pos % GRID_W).astype(np.float32)[:, None] * inv[None, :]
    zero = np.zeros_like(row)
    cos = np.concatenate([np.cos(row), np.cos(row), np.cos(col), np.cos(col)], axis=-1)
    s_up = np.concatenate([-np.sin(row), zero, -np.sin(col), zero], axis=-1)
    s_dn = np.concatenate([zero, np.sin(row), zero, np.sin(col)], axis=-1)
    pad = lambda t, v: np.concatenate([t, np.full((CTX_LEN, HEAD_DIM), v, np.float32)], axis=0).astype(np.float32)
    return pad(cos, 1.0), pad(s_up, 0.0), pad(s_dn, 0.0)


def kernel(x, c, ctx, c_ctx, ada_w, ada_b, pre_g, post_g, ab_w_in, ab_sink, ab_ln_g, ab_ln_b, ab_ws, ab_ws_b,
           ab_w_out, cv_w_in, cv_dw, cv_dw_b, cv_ln_g, cv_ln_b, cv_w_out):
    assert x.shape == (1, SEQ, D_MODEL) and ctx.shape == (1, CTX_LEN, D_MODEL)
    cc = jnp.concatenate([c, c_ctx[None, :], jnp.zeros((6, D_MODEL), F32)], axis=0)
    mod = _modulation(cc, ada_w, ada_b)
    rope_c, rope_s1, rope_s2 = _rope_tables()
    pre_g = pre_g.reshape(DEPTH, 1, D_MODEL)
    post_g = post_g.reshape(DEPTH, 1, D_MODEL)
    vec3 = lambda t: t.reshape(t.shape[0], 1, t.shape[1])

    h = _first_prenorm(x, ctx, mod, pre_g)
    xs = (x, ctx)
    y = None
    for layer in range(DEPTH):
        last = layer == DEPTH - 1
        i = layer // 2
        rows = SEQ if last else ROWS
        tm = TM_LATENT if last else TM_ALL
        tm_wide = TM_WIDE_LATENT if last else TM_WIDE_ALL
        if layer > 0:
            x_all, h = _post_pre(xs, y, mod, post_g, pre_g, layer, rows)
            xs = (x_all,)
        if layer % 2 == 0:
            proj = _inproj_even(h, ab_w_in, i, rope_c, rope_s1, rope_s2, tm)
            sink_b = jnp.broadcast_to(ab_sink[i][:, None], (N_Q_HEADS, HEAD_DIM))
            attn = _attention(proj, sink_b)
            wsb_b = jnp.broadcast_to(ab_ws_b[i][:, :, None], (GMLP_GROUPS, BLOCK, BLOCK))
            mix = _gmlp(proj, ab_ws, wsb_b, vec3(ab_ln_g), vec3(ab_ln_b), i)
            y = _outproj(attn, mix, 0, ab_w_out, i, rows, tm_wide)
        else:
            glu, sg = _inproj_conv(h, cv_w_in, i, tm)
            z = _conv_mix(glu, sg, cv_dw, vec3(cv_dw_b), vec3(cv_ln_g), vec3(cv_ln_b), i, rows)
            y = _outproj(z, z, 1, cv_w_out, i, rows, tm_wide)
    out = _final_post(x_all, y, mod, post_g, DEPTH - 1, SEQ)
    return out[None]
```
